```python
import jax, jax.numpy as jnp
from jax import lax
import numpy as np

D_MODEL = 1024
BATCH = 8
SEQ = 2048
DEPTH = 2
DEC_BATCH = 32
DEC_SEQ = 1
PAST_LEN = 8192
PAGE_SIZE = 128

N_HEADS = 16
HEAD_DIM = D_MODEL // N_HEADS
D_FF = 2816
CONV_WIDTH = 3
N_META = 16
Q_BLOCK = 128
N_MIXERS = 2
N_SB_LAYERS = (DEPTH + 1) // 2
N_FOX_LAYERS = DEPTH // 2
RMS_EPS = 1e-6
ATTN_SCALE = HEAD_DIM ** -0.5

kernel_name = "meta_stickbreak_fox_convffn_step"


def rms_norm(x, g):
    xf = x.astype(jnp.float32)
    y = xf * lax.rsqrt(jnp.mean(xf * xf, axis=-1, keepdims=True) + RMS_EPS)
    return (y * g.astype(jnp.float32)).astype(x.dtype)


def gather_pages(pool, page_table):
    p = pool[page_table]
    return p.reshape((p.shape[0], -1) + p.shape[3:])


def _blocks(a, axis, block):
    n = a.shape[axis]
    nb = -(-n // block)
    pad = nb * block - n
    if pad:
        widths = [(0, 0)] * a.ndim
        widths[axis] = (0, pad)
        a = jnp.pad(a, widths, mode="edge")
    a = a.reshape(a.shape[:axis] + (nb, block) + a.shape[axis + 1:])
    return jnp.moveaxis(a, axis, 0)


def sweep(attend, q_parts, kv, block):
    n = q_parts[0][0].shape[q_parts[0][1]]
    qb = tuple(_blocks(a, ax, block) for a, ax in q_parts)
    out = lax.map(lambda blk: attend(*blk, *kv), qb)
    out = jnp.moveaxis(out, 0, 1)
    out = out.reshape((out.shape[0], -1) + out.shape[3:])
    return out[:, :n]


def causal_sweep(attend, q_arrays, kv_arrays, past_len, segments):
    outs = []
    for start, stop, block in segments:
        n_keys = past_len + stop
        qpos = past_len + jnp.arange(start, stop, dtype=jnp.int32)
        kpos = jnp.arange(n_keys, dtype=jnp.int32)
        q_parts = tuple((a[:, start:stop], 1) for a in q_arrays) + ((qpos, 0),)
        kv = tuple(a[:, :n_keys] for a in kv_arrays) + (kpos,)
        outs.append(sweep(attend, q_parts, kv, block))
    return jnp.concatenate(outs, axis=1) if len(outs) > 1 else outs[0]


def sb_attend(q, qpos, k, v, kpos):
    z = jnp.einsum('bqhd,bkhd->bhqk', q, k).astype(jnp.float32) * ATTN_SCALE
    mask = kpos[None, :] < qpos[:, None]
    log_1m = jnp.where(mask, jax.nn.log_sigmoid(-z), 0.0)
    after = lax.cumsum(log_1m, axis=3, reverse=True) - log_1m
    w = jnp.where(mask, jnp.exp(jax.nn.log_sigmoid(z) + after), 0.0)
    return jnp.einsum('bhqk,bkhd->bqhd', w.astype(v.dtype), v)


def fox_attend(q, qc, qpos, k, v, kc, kpos):
    s = jnp.einsum('bqhd,bkhd->bhqk', q, k).astype(jnp.float32) * ATTN_SCALE
    s = s + jnp.swapaxes(qc, 1, 2)[..., :, None] - jnp.swapaxes(kc, 1, 2)[..., None, :]
    s = jnp.where(kpos[None, :] <= qpos[:, None], s, -jnp.inf)
    p = jax.nn.softmax(s, axis=-1)
    return jnp.einsum('bhqk,bkhd->bqhd', p.astype(v.dtype), v)


def sb_mixer(h, k_past, v_past, w_qkv, w_o, segments):
    B, L, _ = h.shape
    qkv = (h @ w_qkv).reshape(B, L, 3, N_HEADS, HEAD_DIM)
    q, k_new, v_new = qkv[:, :, 0], qkv[:, :, 1], qkv[:, :, 2]
    k = jnp.concatenate([k_past.astype(k_new.dtype), k_new], axis=1)
    v = jnp.concatenate([v_past.astype(v_new.dtype), v_new], axis=1)
    o = causal_sweep(sb_attend, (q,), (k, v), k_past.shape[1], segments)
    return o.reshape(B, L, -1) @ w_o, k_new, v_new


def fox_mixer(h, k_past, v_past, logf_past, w_in, b_f, gq, gk, w_o, segments):
    B, L, _ = h.shape
    P = k_past.shape[1]
    proj = h @ w_in
    qkv = proj[..., :3 * N_HEADS * HEAD_DIM].reshape(B, L, 3, N_HEADS, HEAD_DIM)
    logf_new = jax.nn.log_sigmoid(proj[..., 3 * N_HEADS * HEAD_DIM:].astype(jnp.float32)
                                  + b_f.astype(jnp.float32))
    q = rms_norm(qkv[:, :, 0], gq)
    k_new = rms_norm(qkv[:, :, 1], gk)
    v_new = qkv[:, :, 2]
    k = jnp.concatenate([k_past.astype(k_new.dtype), k_new], axis=1)
    v = jnp.concatenate([v_past.astype(v_new.dtype), v_new], axis=1)
    c = jnp.cumsum(jnp.concatenate([logf_past.astype(jnp.float32), logf_new], axis=1), axis=1)
    o = causal_sweep(fox_attend, (q, c[:, P:]), (k, v, c), P, segments)
    return o.reshape(B, L, -1) @ w_o, k_new, v_new, logf_new


def conv_ffn(h, conv_prev, w_up, conv_w, conv_b, w_down):
    L = h.shape[1]
    u = h @ w_up
    g, val = u[..., :D_FF], u[..., D_FF:]
    gp = jnp.concatenate([conv_prev.astype(g.dtype), g], axis=1)
    gc = conv_b + sum(conv_w[i] * gp[:, i:i + L] for i in range(CONV_WIDTH))
    y = (jax.nn.silu(gc) * val) @ w_down
    return y, gp[:, gp.shape[1] - (CONV_WIDTH - 1):]


def setup_inputs(seed: int = 0) -> dict:
    key = jax.random.key(seed)
    ks = jax.random.split(key, 32)
    n_pages = PAST_LEN // PAGE_SIZE
    n_used = DEC_BATCH * n_pages
    n_pool = n_used + max(1, n_used // 4)
    f32 = jnp.float32
    nrm = lambda k, shape, s=1.0: (jax.random.normal(k, shape, f32) * s)
    hd = N_HEADS * HEAD_DIM
    page_table = jax.random.permutation(ks[0], n_pool)[:n_used].reshape(DEC_BATCH, n_pages).astype(jnp.int32)
    return {
        "x_prompt": nrm(ks[1], (BATCH, SEQ, D_MODEL)),
        "x_sample": nrm(ks[2], (DEC_BATCH, DEC_SEQ, D_MODEL)),
        "cache_sb_k": nrm(ks[3], (N_SB_LAYERS, n_pool, PAGE_SIZE, N_HEADS, HEAD_DIM)),
        "cache_sb_v": nrm(ks[4], (N_SB_LAYERS, n_pool, PAGE_SIZE, N_HEADS, HEAD_DIM)),
        "cache_fox_k": nrm(ks[5], (N_FOX_LAYERS, n_pool, PAGE_SIZE, N_HEADS, HEAD_DIM)),
        "cache_fox_v": nrm(ks[6], (N_FOX_LAYERS, n_pool, PAGE_SIZE, N_HEADS, HEAD_DIM)),
        "cache_fox_logf": jax.nn.log_sigmoid(2.0 + nrm(ks[7], (N_FOX_LAYERS, n_pool, PAGE_SIZE, N_HEADS), 0.5)),
        "state_conv": nrm(ks[8], (DEPTH, DEC_BATCH, CONV_WIDTH - 1, D_FF)),
        "page_table": page_table,
        "meta_tokens": nrm(ks[9], (N_META, D_MODEL)),
        "norm_mix": 1.0 + nrm(ks[10], (DEPTH, D_MODEL), 0.02),
        "norm_ffn": 1.0 + nrm(ks[11], (DEPTH, D_MODEL), 0.02),
        "w_qkv_sb": nrm(ks[12], (N_SB_LAYERS, D_MODEL, 3 * hd), D_MODEL ** -0.5),
        "w_o_sb": nrm(ks[13], (N_SB_LAYERS, hd, D_MODEL), hd ** -0.5),
        "w_in_fox": nrm(ks[14], (N_FOX_LAYERS, D_MODEL, 3 * hd + N_HEADS), D_MODEL ** -0.5),
        "b_forget": 2.0 + nrm(ks[15], (N_FOX_LAYERS, N_HEADS), 0.5),
        "q_norm_g": 1.0 + nrm(ks[16], (N_FOX_LAYERS, HEAD_DIM), 0.02),
        "k_norm_g": 1.0 + nrm(ks[17], (N_FOX_LAYERS, HEAD_DIM), 0.02),
        "w_o_fox": nrm(ks[18], (N_FOX_LAYERS, hd, D_MODEL), hd ** -0.5),
        "w_up": nrm(ks[19], (DEPTH, D_MODEL, 2 * D_FF), D_MODEL ** -0.5),
        "conv_w": nrm(ks[20], (DEPTH, CONV_WIDTH, D_FF), 0.5),
        "conv_b": nrm(ks[21], (DEPTH, D_FF), 0.02),
        "w_down": nrm(ks[22], (DEPTH, D_FF, D_MODEL), D_FF ** -0.5),
        "norm_final": 1.0 + nrm(ks[23], (D_MODEL,), 0.02),
    }


def reference(x_prompt, x_sample, cache_sb_k, cache_sb_v, cache_fox_k, cache_fox_v, cache_fox_logf,
              state_conv, page_table, meta_tokens, norm_mix, norm_ffn, w_qkv_sb, w_o_sb, w_in_fox,
              b_forget, q_norm_g, k_norm_g, w_o_fox, w_up, conv_w, conv_b, w_down, norm_final):
    B = x_prompt.shape[0]
    meta = jnp.broadcast_to(meta_tokens.astype(x_prompt.dtype)[None], (B, N_META, D_MODEL))
    hp = jnp.concatenate([meta, x_prompt], axis=1)
    hs = x_sample
    T = hp.shape[1]
    prompt_segments = ((0, N_META, N_META), (N_META, T, Q_BLOCK))
    sample_segments = ((0, DEC_SEQ, min(Q_BLOCK, DEC_SEQ)),)
    empty_kv = jnp.zeros((B, 0, N_HEADS, HEAD_DIM), hp.dtype)
    empty_f = jnp.zeros((B, 0, N_HEADS), jnp.float32)
    conv_zero = jnp.zeros((B, CONV_WIDTH - 1, D_FF), hp.dtype)

    sbk_p, sbv_p, sbk_s, sbv_s = [], [], [], []
    fk_p, fv_p, ff_p, fk_s, fv_s, ff_s = [], [], [], [], [], []
    conv_p, conv_s = [], []
    for i in range(DEPTH):
        j = i // N_MIXERS
        a_p = rms_norm(hp, norm_mix[i])
        a_s = rms_norm(hs, norm_mix[i])
        if i % N_MIXERS == 0:
            yp, kp, vp = sb_mixer(a_p, empty_kv, empty_kv, w_qkv_sb[j], w_o_sb[j], prompt_segments)
            ys, ks_, vs_ = sb_mixer(a_s, gather_pages(cache_sb_k[j], page_table),
                                    gather_pages(cache_sb_v[j], page_table),
                                    w_qkv_sb[j], w_o_sb[j], sample_segments)
            sbk_p.append(kp); sbv_p.append(vp); sbk_s.append(ks_); sbv_s.append(vs_)
        else:
            yp, kp, vp, fp = fox_mixer(a_p, empty_kv, empty_kv, empty_f, w_in_fox[j], b_forget[j],
                                       q_norm_g[j], k_norm_g[j], w_o_fox[j], prompt_segments)
            ys, ks_, vs_, fs = fox_mixer(a_s, gather_pages(cache_fox_k[j], page_table),
                                         gather_pages(cache_fox_v[j], page_table),
                                         gather_pages(cache_fox_logf[j], page_table),
                                         w_in_fox[j], b_forget[j], q_norm_g[j], k_norm_g[j],
                                         w_o_fox[j], sample_segments)
            fk_p.append(kp); fv_p.append(vp); ff_p.append(fp)
            fk_s.append(ks_); fv_s.append(vs_); ff_s.append(fs)
        hp = hp + yp
        hs = hs + ys
        fp_out, cp = conv_ffn(rms_norm(hp, norm_ffn[i]), conv_zero, w_up[i], conv_w[i], conv_b[i], w_down[i])
        fs_out, cs = conv_ffn(rms_norm(hs, norm_ffn[i]), state_conv[i], w_up[i], conv_w[i], conv_b[i], w_down[i])
        hp = hp + fp_out
        hs = hs + fs_out
        conv_p.append(cp); conv_s.append(cs)

    y_prompt = rms_norm(hp[:, N_META:], norm_final)
    y_sample = rms_norm(hs, norm_final)
    return (y_prompt, y_sample,
            jnp.stack(sbk_p), jnp.stack(sbv_p), jnp.stack(fk_p), jnp.stack(fv_p), jnp.stack(ff_p),
            jnp.stack(conv_p),
            jnp.stack(sbk_s), jnp.stack(sbv_s), jnp.stack(fk_s), jnp.stack(fv_s), jnp.stack(ff_s),
            jnp.stack(conv_s))
```

```python
import functools

import jax
import jax.numpy as jnp
from jax import lax
from jax.experimental import pallas as pl
from jax.experimental.pallas import tpu as pltpu

F32 = jnp.float32
BF16 = jnp.bfloat16

N_HEADS = 16
HEAD_DIM = 64
HD = N_HEADS * HEAD_DIM
N_META = 16
CONV_WIDTH = 3
RMS_EPS = 1e-6
ATTN_SCALE = HEAD_DIM ** -0.5

LANES = 128
SUBLANES = 8
BF16_ROWS = 16
MXU_DIM = 256
BLK = LANES
PAIR = 2 * HEAD_DIM
V7X_SCOPED_VMEM_BYTES = 60000 * 1024
NEG_BIG = -1e30

ROW_TILE_CAP = 688
FFN_CHUNK = MXU_DIM
DECODE_PAGES_PER_STEP = 8


def _row_tile(m, cap=ROW_TILE_CAP):
  best = None
  for t in range(BF16_ROWS, min(m, cap) + 1, BF16_ROWS):
    if m % t == 0:
      best = t
  assert best is not None, m
  return best


def _params(n_axes, vmem_bytes):
  assert vmem_bytes <= V7X_SCOPED_VMEM_BYTES
  return pltpu.CompilerParams(
      dimension_semantics=("arbitrary",) * n_axes, vmem_limit_bytes=int(vmem_bytes))


def _resident(shape):
  zeros = (0,) * len(shape)
  return pl.BlockSpec(shape, lambda *_: zeros, pipeline_mode=pl.Buffered(1))


def _rms(x, g):
  ms = jnp.mean(x * x, axis=-1, keepdims=True)
  return x * lax.rsqrt(ms + RMS_EPS) * g


def _neg_softplus(z):
  return -(jnp.maximum(z, 0.0) + jnp.log(1.0 + jnp.exp(-jnp.abs(z))))


def _split2(x):
  hi = x.astype(BF16)
  lo = (x - hi.astype(F32)).astype(BF16)
  return hi, lo


def _split3(x):
  hi = x.astype(BF16)
  r = x - hi.astype(F32)
  mid = r.astype(BF16)
  lo = (r - mid.astype(F32)).astype(BF16)
  return hi, mid, lo


def _dot(a, b):
  return jnp.dot(a, b, preferred_element_type=F32)


def _dot_nt(a, b):
  return lax.dot_general(a, b, (((1,), (1,)), ((), ())), preferred_element_type=F32)


def _dot_tn(a, b):
  return lax.dot_general(a, b, (((0,), (0,)), ((), ())), preferred_element_type=F32)


def _proj_sb_kernel(x_ref, g_ref, w_ref, q_ref, k_ref, v_ref):
  xn = _rms(x_ref[...], g_ref[...]).astype(BF16)
  q_ref[...] = (_dot(xn, w_ref[:, 0:HD]) * ATTN_SCALE).astype(BF16)
  k_ref[...] = _dot(xn, w_ref[:, HD:2 * HD])
  v_ref[...] = _dot(xn, w_ref[:, 2 * HD:3 * HD])


def _proj_sb(x2d, g, w):
  m, d = x2d.shape
  tm = _row_tile(m)
  row = lambda i: (i, 0)
  vmem = 2 * tm * (d * 4 + HD * 2 + 2 * HD * 4) + d * 3 * HD * 2 + 6 * tm * HD * 4
  return pl.pallas_call(
      _proj_sb_kernel,
      grid=(m // tm,),
      in_specs=[pl.BlockSpec((tm, d), row), _resident((1, d)), _resident((d, 3 * HD))],
      out_specs=[pl.BlockSpec((tm, HD), row)] * 3,
      out_shape=[jax.ShapeDtypeStruct((m, HD), BF16),
                 jax.ShapeDtypeStruct((m, HD), F32),
                 jax.ShapeDtypeStruct((m, HD), F32)],
      compiler_params=_params(1, vmem + (4 << 20)),
      name="proj_sb",
  )(x2d, g.reshape(1, d), w)


def _proj_fox_kernel(x_ref, g_ref, w_ref, wf_ref, bf_ref, gq_ref, gk_ref, s_ref,
                     q_ref, k_ref, v_ref, lf_ref):
  xn = _rms(x_ref[...], g_ref[...]).astype(BF16)

  def head_norm(y, gain):
    y2 = (y * y).astype(BF16)
    ss = jnp.concatenate(
        [_dot(y2[:, c * MXU_DIM:(c + 1) * MXU_DIM], s_ref[...]) for c in range(HD // MXU_DIM)],
        axis=1)
    return y * lax.rsqrt(ss * (1.0 / HEAD_DIM) + RMS_EPS) * gain

  q = head_norm(_dot(xn, w_ref[:, 0:HD]), gq_ref[...])
  q_ref[...] = (q * ATTN_SCALE).astype(BF16)
  k_ref[...] = head_norm(_dot(xn, w_ref[:, HD:2 * HD]), gk_ref[...])
  v_ref[...] = _dot(xn, w_ref[:, 2 * HD:3 * HD])
  f = _dot(xn, wf_ref[...])[:, 0:N_HEADS] + bf_ref[...]
  lf_ref[...] = jnp.minimum(f, 0.0) - jnp.log(1.0 + jnp.exp(-jnp.abs(f)))


def _proj_fox(x2d, g, w, wf, bf, gq, gk):
  m, d = x2d.shape
  tm = _row_tile(m)
  row = lambda i: (i, 0)
  r = jnp.arange(MXU_DIM) // HEAD_DIM
  seg = (r[:, None] == r[None, :]).astype(BF16)
  vmem = 2 * tm * (d * 4 + HD * 2 + 2 * HD * 4 + LANES * 4) + d * (3 * HD + LANES) * 2 + 8 * tm * HD * 4
  return pl.pallas_call(
      _proj_fox_kernel,
      grid=(m // tm,),
      in_specs=[pl.BlockSpec((tm, d), row), _resident((1, d)), _resident((d, 3 * HD)),
                _resident((d, LANES)), _resident((1, N_HEADS)), _resident((1, HD)),
                _resident((1, HD)), _resident((MXU_DIM, MXU_DIM))],
      out_specs=[pl.BlockSpec((tm, HD), row)] * 3 + [pl.BlockSpec((tm, N_HEADS), row)],
      out_shape=[jax.ShapeDtypeStruct((m, HD), BF16),
                 jax.ShapeDtypeStruct((m, HD), F32),
                 jax.ShapeDtypeStruct((m, HD), F32),
                 jax.ShapeDtypeStruct((m, N_HEADS), F32)],
      compiler_params=_params(1, min(vmem + (4 << 20), V7X_SCOPED_VMEM_BYTES)),
      name="proj_fox",
  )(x2d, g.reshape(1, d), w, wf, bf.reshape(1, N_HEADS),
    jnp.tile(gq, N_HEADS).reshape(1, HD), jnp.tile(gk, N_HEADS).reshape(1, HD), seg)


def _stage_kv(k_ref, v_ref, kab, vab, t_len):
  nfull, tail = divmod(t_len, BLK)

  def put(dst, d0, rows, x):
    head0 = lax.broadcasted_iota(jnp.int32, (rows, PAIR), 1) < HEAD_DIM
    dst[pl.ds(d0, rows), :] = jnp.where(head0, x, 0.0).astype(BF16)
    dst[pl.ds(pl.multiple_of(d0 + BLK, BLK), rows), :] = jnp.where(head0, 0.0, x).astype(BF16)

  def body(j, c):
    r0 = pl.multiple_of(j * BLK, BLK)
    d0 = pl.multiple_of(j * 2 * BLK, BLK)
    put(kab, d0, BLK, k_ref[0, pl.ds(r0, BLK), :])
    put(vab, d0, BLK, v_ref[0, pl.ds(r0, BLK), :])
    return c

  lax.fori_loop(0, nfull, body, 0)
  if tail:
    d0 = nfull * 2 * BLK
    kab[d0:d0 + 2 * BLK, :] = jnp.zeros((2 * BLK, PAIR), BF16)
    vab[d0:d0 + 2 * BLK, :] = jnp.zeros((2 * BLK, PAIR), BF16)
    put(kab, d0, tail, k_ref[0, nfull * BLK:t_len, :])
    put(vab, d0, tail, v_ref[0, nfull * BLK:t_len, :])


def _for_each_qblock(t_len, fn):
  nfull, tail = divmod(t_len, BLK)

  def body(i, c):
    fn(pl.multiple_of(i * BLK, BLK), BLK, i)
    return c

  lax.fori_loop(0, nfull, body, 0)
  if tail:
    fn(nfull * BLK, tail, nfull)


def _sb_attn_kernel(q_ref, k_ref, v_ref, uw_ref, o_ref, kab, vab, r_ref, acc_ref, *, t_len):
  _stage_kv(k_ref, v_ref, kab, vab, t_len)

  def qblock(q0, tq, jdiag):
    qb = q_ref[0, pl.ds(q0, tq), :]
    acc_ref[0:tq, :] = jnp.zeros((tq, PAIR), F32)
    r_ref[0:tq, :] = jnp.zeros((tq, 2 * BLK), F32)
    row = lax.broadcasted_iota(jnp.int32, (tq, BLK), 0)
    col = lax.broadcasted_iota(jnp.int32, (tq, BLK), 1)
    below = col < row
    below2 = jnp.concatenate([below, below], axis=1)

    def kv_step(j, diagonal):
      d0 = pl.multiple_of(j * 2 * BLK, 2 * BLK)
      z = _dot_nt(qb, kab[pl.ds(d0, 2 * BLK), :])
      l1m = _neg_softplus(z)
      if diagonal:
        l1m = jnp.where(below2, l1m, 0.0)
      hi, lo = _split2(l1m)
      r_old = r_ref[0:tq, :]
      a_parts = []
      for h in range(2):
        sl = slice(h * BLK, (h + 1) * BLK)
        ct = _dot(jnp.concatenate([hi[:, sl], lo[:, sl]], axis=1), uw_ref[...])
        a = jnp.exp(z[:, sl] + ct[:, 0:BLK] + r_old[:, sl])
        if diagonal:
          a = jnp.where(below, a, 0.0)
        a_parts.append(a)
        r_ref[0:tq, sl] = r_old[:, sl] + ct[:, BLK:2 * BLK]
      a2 = jnp.concatenate(a_parts, axis=1).astype(BF16)
      acc_ref[0:tq, :] += _dot(a2, vab[pl.ds(d0, 2 * BLK), :])

    kv_step(jdiag, True)

    def body(t, c):
      kv_step(jdiag - 1 - t, False)
      return c

    lax.fori_loop(0, jdiag, body, 0)
    o_ref[0, pl.ds(q0, tq), :] = acc_ref[0:tq, :].astype(BF16)

  _for_each_qblock(t_len, qblock)


def _suffix_sum_weights():
  j = jnp.arange(BLK)
  u = (j[:, None] >= j[None, :]).astype(BF16)
  half = jnp.concatenate([u, jnp.ones((BLK, BLK), BF16)], axis=1)
  return jnp.concatenate([half, half], axis=0)


def _kv_blocks(t_len):
  return -(-t_len // BLK)


def _sb_attn_prompt(q, k, v):
  b, t_len, _ = q.shape
  nb = _kv_blocks(t_len)
  blk = pl.BlockSpec((1, t_len, PAIR), lambda i, p: (i, 0, p))
  return pl.pallas_call(
      functools.partial(_sb_attn_kernel, t_len=t_len),
      grid=(b, HD // PAIR),
      in_specs=[blk, blk, blk, _resident((2 * BLK, 2 * BLK))],
      out_specs=blk,
      out_shape=jax.ShapeDtypeStruct((b, t_len, HD), BF16),
      scratch_shapes=[pltpu.VMEM((nb * 2 * BLK, PAIR), BF16),
                      pltpu.VMEM((nb * 2 * BLK, PAIR), BF16),
                      pltpu.VMEM((BLK, 2 * BLK), F32),
                      pltpu.VMEM((BLK, PAIR), F32)],
      compiler_params=_params(2, 32 << 20),
      name="sb_attn_prompt",
  )(q, k, v, _suffix_sum_weights())


def _fox_attn_kernel(q_ref, k_ref, v_ref, lf_ref, ltri_ref, o_ref,
                     kab, vab, cq, ckt, m_ref, l_ref, acc_ref, *, t_len):
  _stage_kv(k_ref, v_ref, kab, vab, t_len)
  nfull, tail = divmod(t_len, BLK)
  nb = nfull + (1 if tail else 0)

  carry = jnp.zeros((1, 2 * BLK), F32)
  for j in range(nb):
    rows = BLK if j < nfull else tail
    lf = lf_ref[0, 0, j * BLK:j * BLK + rows, :]
    x = jnp.concatenate([jnp.broadcast_to(lf[:, h:h + 1], (rows, BLK)) for h in range(2)], axis=1)
    if rows < BLK:
      x = jnp.concatenate([x, jnp.zeros((BLK - rows, 2 * BLK), F32)], axis=0)
    c = carry
    for part in _split3(x):
      c = c + _dot(ltri_ref[...], part)
    cq[j * BLK:(j + 1) * BLK, :] = c
    carry = c[BLK - 1:BLK, :]
  for j in range(nb):
    cb = cq[j * BLK:(j + 1) * BLK, :]
    ckt[j] = jnp.concatenate([cb[:, 0:BLK].T[0:SUBLANES], cb[:, BLK:2 * BLK].T[0:SUBLANES]], axis=1)

  def qblock(q0, tq, jdiag):
    head0 = lax.broadcasted_iota(jnp.int32, (tq, PAIR), 1) < HEAD_DIM
    qb = q_ref[0, pl.ds(q0, tq), :]
    cqb = cq[pl.ds(q0, tq), :]
    acc_ref[0:tq, :] = jnp.zeros((tq, PAIR), F32)
    m_ref[0:tq, :] = jnp.full((tq, 2 * BLK), NEG_BIG, F32)
    l_ref[0:tq, :] = jnp.zeros((tq, 2 * BLK), F32)
    row = lax.broadcasted_iota(jnp.int32, (tq, BLK), 0)
    col = lax.broadcasted_iota(jnp.int32, (tq, BLK), 1)
    upto = col <= row
    upto2 = jnp.concatenate([upto, upto], axis=1)

    def kv_step(j, diagonal):
      d0 = pl.multiple_of(j * 2 * BLK, 2 * BLK)
      s = _dot_nt(qb, kab[pl.ds(d0, 2 * BLK), :]) + cqb - ckt[j][0:1, :]
      if diagonal:
        s = jnp.where(upto2, s, NEG_BIG)
      m_old = m_ref[0:tq, :]
      l_old = l_ref[0:tq, :]
      p_parts, alphas = [], []
      for h in range(2):
        sl = slice(h * BLK, (h + 1) * BLK)
        m_new = jnp.maximum(m_old[:, sl], jnp.max(s[:, sl], axis=1, keepdims=True))
        p = jnp.exp(s[:, sl] - m_new)
        alpha = jnp.exp(m_old[:, sl] - m_new)
        m_ref[0:tq, sl] = m_new
        l_ref[0:tq, sl] = alpha * l_old[:, sl] + jnp.sum(p, axis=1, keepdims=True)
        p_parts.append(p)
        alphas.append(alpha)
      p2 = jnp.concatenate(p_parts, axis=1).astype(BF16)
      scale = jnp.where(head0, alphas[0], alphas[1])
      acc_ref[0:tq, :] = acc_ref[0:tq, :] * scale + _dot(p2, vab[pl.ds(d0, 2 * BLK), :])

    kv_step(jdiag, True)

    def body(t, c):
      kv_step(jdiag - 1 - t, False)
      return c

    lax.fori_loop(0, jdiag, body, 0)
    l_fin = l_ref[0:tq, :]
    denom = jnp.where(head0, l_fin[:, 0:BLK], l_fin[:, BLK:2 * BLK])
    o_ref[0, pl.ds(q0, tq), :] = (acc_ref[0:tq, :] / denom).astype(BF16)

  _for_each_qblock(t_len, qblock)


def _fox_attn_prompt(q, k, v, logf):
  b, t_len, _ = q.shape
  nb = _kv_blocks(t_len)
  n_pairs = HD // PAIR
  lf = logf.reshape(b, t_len, n_pairs, 2).transpose(0, 2, 1, 3)
  j = jnp.arange(BLK)
  ltri = (j[:, None] >= j[None, :]).astype(BF16)
  blk = pl.BlockSpec((1, t_len, PAIR), lambda i, p: (i, 0, p))
  return pl.pallas_call(
      functools.partial(_fox_attn_kernel, t_len=t_len),
      grid=(b, n_pairs),
      in_specs=[blk, blk, blk,
                pl.BlockSpec((1, 1, t_len, 2), lambda i, p: (i, p, 0, 0)),
                _resident((BLK, BLK))],
      out_specs=blk,
      out_shape=jax.ShapeDtypeStruct((b, t_len, HD), BF16),
      scratch_shapes=[pltpu.VMEM((nb * 2 * BLK, PAIR), BF16),
                      pltpu.VMEM((nb * 2 * BLK, PAIR), BF16),
                      pltpu.VMEM((nb * BLK, 2 * BLK), F32),
                      pltpu.VMEM((nb, SUBLANES, 2 * BLK), F32),
                      pltpu.VMEM((BLK, 2 * BLK), F32),
                      pltpu.VMEM((BLK, 2 * BLK), F32),
                      pltpu.VMEM((BLK, PAIR), F32)],
      compiler_params=_params(2, 40 << 20),
      name="fox_attn_prompt",
  )(q, k, v, lf, ltri)


def _silu(x):
  return x * (1.0 / (1.0 + jnp.exp(-x)))


def _post_prompt_kernel(h_ref, o_ref, wo_ref, g_ref, wg_ref, wv_ref, cw_ref, cb_ref, wd_ref,
                        *rest, final):
  if final:
    gfin_ref, hout_ref, conv_ref, y_ref, gbuf, carry = rest
  else:
    hout_ref, conv_ref, gbuf, carry = rest
  t = pl.program_id(1)
  tm = h_ref.shape[1]
  n_chunks = wg_ref.shape[0]
  pad = SUBLANES

  @pl.when(t == 0)
  def _():
    carry[...] = jnp.zeros(carry.shape, F32)

  h1 = h_ref[0] + _dot(o_ref[0], wo_ref[...])
  hn = _rms(h1, g_ref[...]).astype(BF16)
  hout_ref[0] = h1

  def chunk(c, _):
    g = _dot(hn, wg_ref[c])
    val = _dot(hn, wv_ref[c])
    gbuf[0:pad, :] = carry[c]
    gbuf[pad:pad + tm, :] = g
    w = cw_ref[c]
    gc = (cb_ref[c] + w[0:1] * gbuf[pad - 2:pad - 2 + tm, :]
          + w[1:2] * gbuf[pad - 1:pad - 1 + tm, :] + w[2:3] * g)
    act = (_silu(gc) * val).astype(BF16)
    hout_ref[0] += _dot(act, wd_ref[c])
    last = gbuf[tm:tm + pad, :]
    carry[c] = last
    conv_ref[0, c] = last[pad - (CONV_WIDTH - 1):pad, :]
    return _

  lax.fori_loop(0, n_chunks, chunk, 0)
  if final:
    y_ref[0] = _rms(hout_ref[0], gfin_ref[...])


def _ffn_weights(w_up, conv_w, conv_b, w_down):
  d, two_ff = w_up.shape
  d_ff = two_ff // 2
  assert d_ff % FFN_CHUNK == 0
  nc = d_ff // FFN_CHUNK
  wg = w_up[:, :d_ff].astype(BF16).reshape(d, nc, FFN_CHUNK).transpose(1, 0, 2)
  wv = w_up[:, d_ff:].astype(BF16).reshape(d, nc, FFN_CHUNK).transpose(1, 0, 2)
  wd = w_down.astype(BF16).reshape(nc, FFN_CHUNK, d)
  cw = conv_w.reshape(CONV_WIDTH, nc, FFN_CHUNK).transpose(1, 0, 2)
  cb = conv_b.reshape(nc, 1, FFN_CHUNK)
  return wg, wv, cw, cb, wd


def _post_prompt(h, o, wo, g, ffn, g_final=None):
  b, t_len, d = h.shape
  wg, wv, cw, cb, wd = ffn
  nc = wg.shape[0]
  final = g_final is not None
  tm = _row_tile(t_len, cap=ROW_TILE_CAP // 2)
  tile = lambda i, t: (i, t, 0)
  act_spec = pl.BlockSpec((1, tm, d), tile)
  act_shape = jax.ShapeDtypeStruct((b, t_len, d), F32)
  w_bytes = 2 * (wo.size + wg.size + wv.size + wd.size)
  vmem = w_bytes + 2 * tm * d * (4 + 2 + 4 + 4) + 10 * tm * d * 4
  outs = pl.pallas_call(
      functools.partial(_post_prompt_kernel, final=final),
      grid=(b, t_len // tm),
      in_specs=[act_spec, pl.BlockSpec((1, tm, HD), tile), _resident(wo.shape), _resident((1, d)),
                _resident(wg.shape), _resident(wv.shape), _resident(cw.shape), _resident(cb.shape),
                _resident(wd.shape)] + ([_resident((1, d))] if final else []),
      out_specs=[act_spec,
                 pl.BlockSpec((1, nc, CONV_WIDTH - 1, FFN_CHUNK), lambda i, t: (i, 0, 0, 0))]
      + ([act_spec] if final else []),
      out_shape=[act_shape, jax.ShapeDtypeStruct((b, nc, CONV_WIDTH - 1, FFN_CHUNK), F32)]
      + ([act_shape] if final else []),
      scratch_shapes=[pltpu.VMEM((tm + SUBLANES, FFN_CHUNK), F32),
                      pltpu.VMEM((nc, SUBLANES, FFN_CHUNK), F32)],
      compiler_params=_params(2, min(vmem, V7X_SCOPED_VMEM_BYTES)),
      name="post_prompt_final" if final else "post_prompt",
  )(h, o, wo, g.reshape(1, d), wg, wv, cw, cb, wd, *([g_final.reshape(1, d)] if final else []))
  conv = outs[1].transpose(0, 2, 1, 3).reshape(b, CONV_WIDTH - 1, nc * FFN_CHUNK)
  return (outs[0], conv) + tuple(outs[2:])


def _post_sample_kernel(h_ref, o_ref, wo_ref, g_ref, wg_ref, wv_ref, cw_ref, cb_ref, wd_ref,
                        prev_ref, *rest, final):
  if final:
    gfin_ref, hout_ref, conv_ref, y_ref = rest
  else:
    hout_ref, conv_ref = rest
  n_chunks = wg_ref.shape[0]
  h1 = h_ref[...] + _dot(o_ref[...], wo_ref[...])
  hn = _rms(h1, g_ref[...]).astype(BF16)
  acc = h1
  for c in range(n_chunks):
    g = _dot(hn, wg_ref[c])
    val = _dot(hn, wv_ref[c])
    w = cw_ref[c]
    prev0 = prev_ref[0, c]
    prev1 = prev_ref[1, c]
    gc = cb_ref[c] + w[0:1] * prev0 + w[1:2] * prev1 + w[2:3] * g
    acc = acc + _dot((_silu(gc) * val).astype(BF16), wd_ref[c])
    conv_ref[0, c] = prev1
    conv_ref[1, c] = g
  hout_ref[...] = acc
  if final:
    y_ref[...] = _rms(acc, gfin_ref[...])


def _post_sample(h, o, wo, g, ffn, state, g_final=None):
  n, d = h.shape
  wg, wv, cw, cb, wd = ffn
  nc = wg.shape[0]
  final = g_final is not None
  prev = state.reshape(n, CONV_WIDTH - 1, nc, FFN_CHUNK).transpose(1, 2, 0, 3)
  full = lambda shape: pl.BlockSpec(shape, lambda i: (0,) * len(shape))
  row_shape = jax.ShapeDtypeStruct((n, d), F32)
  w_bytes = 2 * (wo.size + wg.size + wv.size + wd.size)
  outs = pl.pallas_call(
      functools.partial(_post_sample_kernel, final=final),
      grid=(1,),
      in_specs=[full((n, d)), full((n, HD)), _resident(wo.shape), full((1, d)),
                _resident(wg.shape), _resident(wv.shape), full(cw.shape), full(cb.shape),
                _resident(wd.shape), full(prev.shape)] + ([full((1, d))] if final else []),
      out_specs=[full((n, d)), full(prev.shape)] + ([full((n, d))] if final else []),
      out_shape=[row_shape, jax.ShapeDtypeStruct(prev.shape, F32)] + ([row_shape] if final else []),
      compiler_params=_params(1, min(w_bytes + (12 << 20), V7X_SCOPED_VMEM_BYTES)),
      name="post_sample_final" if final else "post_sample",
  )(h, o, wo, g.reshape(1, d), wg, wv, cw, cb, wd, prev, *([g_final.reshape(1, d)] if final else []))
  conv = outs[1].transpose(2, 0, 1, 3).reshape(n, CONV_WIDTH - 1, nc * FFN_CHUNK)
  return (outs[0], conv) + tuple(outs[2:])


def _stage_qbd(q_row, qbd):
  lane = lax.broadcasted_iota(jnp.int32, (N_HEADS, HD), 1)
  head = lax.broadcasted_iota(jnp.int32, (N_HEADS, HD), 0)
  own = (lane // HEAD_DIM) == head
  qbd[...] = jnp.where(own, jnp.broadcast_to(q_row.astype(F32), (N_HEADS, HD)), 0.0).astype(BF16)
  return own


def _own_lanes():
  lane = lax.broadcasted_iota(jnp.int32, (N_HEADS, HD), 1)
  head = lax.broadcasted_iota(jnp.int32, (N_HEADS, HD), 0)
  return (lane // HEAD_DIM) == head


def _sb_decode_kernel(pt_ref, q_ref, uw_ref, *refs, pages):
  k_refs = refs[0:pages]
  v_refs = refs[pages:2 * pages]
  o_ref, qbd, r_ref, acc_ref = refs[2 * pages:]
  g = pl.program_id(1)

  @pl.when(g == 0)
  def _():
    _stage_qbd(q_ref[0], qbd)
    r_ref[...] = jnp.zeros(r_ref.shape, F32)
    acc_ref[...] = jnp.zeros(acc_ref.shape, F32)

  for i in range(pages):
    z = _dot_nt(qbd[...], k_refs[i][0].astype(BF16))
    hi, lo = _split2(_neg_softplus(z))
    ct = _dot(jnp.concatenate([hi, lo], axis=1), uw_ref[...])
    r_old = r_ref[...]
    a = jnp.exp(z + ct[:, 0:BLK] + r_old)
    r_ref[...] = r_old + ct[:, BLK:2 * BLK]
    acc_ref[...] += _dot(a.astype(BF16), v_refs[i][0].astype(BF16))

  @pl.when(g == pl.num_programs(1) - 1)
  def _():
    o_ref[0] = jnp.sum(jnp.where(_own_lanes(), acc_ref[...], 0.0), axis=0, keepdims=True).astype(BF16)


def _page_specs(n_pages, pages, page_size, width):
  def spec(i):
    return pl.BlockSpec(
        (1, page_size, width),
        lambda b, g, pt: (pt[b, n_pages - 1 - (g * pages + i)], 0, 0))
  return [spec(i) for i in range(pages)]


def _sb_attn_decode(q, k_pool, v_pool, page_table):
  n = q.shape[0]
  n_pool, page_size = k_pool.shape[0], k_pool.shape[1]
  assert page_size == BLK
  n_pages = page_table.shape[1]
  pages = DECODE_PAGES_PER_STEP
  assert n_pages % pages == 0
  kp = k_pool.reshape(n_pool, page_size, HD)
  vp = v_pool.reshape(n_pool, page_size, HD)
  row = pl.BlockSpec((1, 1, HD), lambda b, g, pt: (b, 0, 0))
  grid_spec = pltpu.PrefetchScalarGridSpec(
      num_scalar_prefetch=1,
      grid=(n, n_pages // pages),
      in_specs=[row, pl.BlockSpec((2 * BLK, 2 * BLK), lambda b, g, pt: (0, 0))]
      + _page_specs(n_pages, pages, page_size, HD) * 2,
      out_specs=row,
      scratch_shapes=[pltpu.VMEM((N_HEADS, HD), BF16),
                      pltpu.VMEM((N_HEADS, BLK), F32),
                      pltpu.VMEM((N_HEADS, HD), F32)],
  )
  o = pl.pallas_call(
      functools.partial(_sb_decode_kernel, pages=pages),
      grid_spec=grid_spec,
      out_shape=jax.ShapeDtypeStruct((n, 1, HD), BF16),
      compiler_params=_params(2, 2 * 2 * pages * page_size * HD * 4 + (12 << 20)),
      name="sb_attn_decode",
  )(page_table, q.reshape(n, 1, HD), _suffix_sum_weights(), *([kp] * pages), *([vp] * pages))
  return o.reshape(n, HD)


def _fox_decode_kernel(pt_ref, q_ref, kn_ref, vn_ref, lfn_ref, us_ref, *refs, pages):
  k_refs = refs[0:pages]
  v_refs = refs[pages:2 * pages]
  f_refs = refs[2 * pages:3 * pages]
  o_ref, qbd, d_ref, m_ref, l_ref, acc_ref = refs[3 * pages:]
  g = pl.program_id(1)

  @pl.when(g == 0)
  def _():
    own = _stage_qbd(q_ref[0], qbd)
    s_new = jnp.sum(jnp.where(own, qbd[...].astype(F32) * kn_ref[0], 0.0), axis=1, keepdims=True)
    m_ref[...] = jnp.broadcast_to(s_new, m_ref.shape)
    l_ref[...] = jnp.ones(l_ref.shape, F32)
    acc_ref[...] = jnp.broadcast_to(vn_ref[0], acc_ref.shape)
    d_ref[...] = jnp.broadcast_to(lfn_ref[0], d_ref.shape)

  for i in range(pages):
    dt = jnp.zeros((N_HEADS, 2 * BLK), F32)
    for part in _split3(f_refs[i][0]):
      dt = dt + _dot_tn(part, us_ref[...])
    d_old = d_ref[...]
    s = _dot_nt(qbd[...], k_refs[i][0].astype(BF16)) + dt[:, 0:BLK] + d_old
    d_ref[...] = d_old + dt[:, BLK:2 * BLK]
    m_old = m_ref[...]
    m_new = jnp.maximum(m_old, jnp.max(s, axis=1, keepdims=True))
    p = jnp.exp(s - m_new)
    alpha = jnp.exp(m_old - m_new)
    m_ref[...] = m_new
    l_ref[...] = alpha * l_ref[...] + jnp.sum(p, axis=1, keepdims=True)
    scale = jnp.concatenate([alpha] * (HD // BLK), axis=1)
    acc_ref[...] = acc_ref[...] * scale + _dot(p.astype(BF16), v_refs[i][0].astype(BF16))

  @pl.when(g == pl.num_programs(1) - 1)
  def _():
    inv = jnp.concatenate([1.0 / l_ref[...]] * (HD // BLK), axis=1)
    o_ref[0] = jnp.sum(jnp.where(_own_lanes(), acc_ref[...] * inv, 0.0),
                       axis=0, keepdims=True).astype(BF16)


def _fox_attn_decode(q, k_new, v_new, lf_new, k_pool, v_pool, f_pool, page_table):
  n = q.shape[0]
  n_pool, page_size = k_pool.shape[0], k_pool.shape[1]
  assert page_size == BLK
  n_pages = page_table.shape[1]
  pages = DECODE_PAGES_PER_STEP
  assert n_pages % pages == 0
  kp = k_pool.reshape(n_pool, page_size, HD)
  vp = v_pool.reshape(n_pool, page_size, HD)
  j = jnp.arange(BLK)
  us = jnp.concatenate([(j[:, None] > j[None, :]).astype(BF16), jnp.ones((BLK, BLK), BF16)], axis=1)
  row = pl.BlockSpec((1, 1, HD), lambda b, g, pt: (b, 0, 0))
  grid_spec = pltpu.PrefetchScalarGridSpec(
      num_scalar_prefetch=1,
      grid=(n, n_pages // pages),
      in_specs=[row, row, row,
                pl.BlockSpec((1, N_HEADS, 1), lambda b, g, pt: (b, 0, 0)),
                pl.BlockSpec((BLK, 2 * BLK), lambda b, g, pt: (0, 0))]
      + _page_specs(n_pages, pages, page_size, HD) * 2
      + _page_specs(n_pages, pages, page_size, N_HEADS),
      out_specs=row,
      scratch_shapes=[pltpu.VMEM((N_HEADS, HD), BF16),
                      pltpu.VMEM((N_HEADS, BLK), F32),
                      pltpu.VMEM((N_HEADS, BLK), F32),
                      pltpu.VMEM((N_HEADS, BLK), F32),
                      pltpu.VMEM((N_HEADS, HD), F32)],
  )
  o = pl.pallas_call(
      functools.partial(_fox_decode_kernel, pages=pages),
      grid_spec=grid_spec,
      out_shape=jax.ShapeDtypeStruct((n, 1, HD), BF16),
      compiler_params=_params(2, 2 * 2 * pages * page_size * (HD + LANES) * 4 + (12 << 20)),
      name="fox_attn_decode",
  )(page_table, q.reshape(n, 1, HD), k_new.reshape(n, 1, HD), v_new.reshape(n, 1, HD),
    lf_new.reshape(n, N_HEADS, 1), us, *([kp] * pages), *([vp] * pages), *([f_pool] * pages))
  return o.reshape(n, HD)


def kernel(x_prompt, x_sample, cache_sb_k, cache_sb_v, cache_fox_k, cache_fox_v, cache_fox_logf,
           state_conv, page_table, meta_tokens, norm_mix, norm_ffn, w_qkv_sb, w_o_sb, w_in_fox,
           b_forget, q_norm_g, k_norm_g, w_o_fox, w_up, conv_w, conv_b, w_down, norm_final):
  b, seq, d = x_prompt.shape
  n = x_sample.shape[0]
  assert x_sample.shape[1] == 1 and d == HD
  t_len = N_META + seq
  depth = norm_mix.shape[0]
  assert depth == 2 and w_qkv_sb.shape[0] == 1 and w_in_fox.shape[0] == 1

  meta = jnp.broadcast_to(meta_tokens.astype(x_prompt.dtype)[None], (b, N_META, d))
  hp = jnp.concatenate([meta, x_prompt], axis=1)
  hs = x_sample.reshape(n, d)

  heads = lambda a, lead: a.reshape(lead + (N_HEADS, HEAD_DIM))[None]
  ffn = [_ffn_weights(w_up[i], conv_w[i], conv_b[i], w_down[i]) for i in range(depth)]
  conv_p, conv_s = [], []

  w = w_qkv_sb[0].astype(BF16)
  wo = w_o_sb[0].astype(BF16)
  q, k, v = _proj_sb(hp.reshape(b * t_len, d), norm_mix[0], w)
  sbk_p, sbv_p = heads(k, (b, t_len)), heads(v, (b, t_len))
  o = _sb_attn_prompt(q.reshape(b, t_len, HD), k.reshape(b, t_len, HD), v.reshape(b, t_len, HD))
  hp, cp = _post_prompt(hp, o, wo, norm_ffn[0], ffn[0])
  conv_p.append(cp)

  q, k, v = _proj_sb(hs, norm_mix[0], w)
  sbk_s, sbv_s = heads(k, (n, 1)), heads(v, (n, 1))
  o = _sb_attn_decode(q, cache_sb_k[0], cache_sb_v[0], page_table)
  hs, cs = _post_sample(hs, o, wo, norm_ffn[0], ffn[0], state_conv[0])
  conv_s.append(cs)

  w = w_in_fox[0, :, :3 * HD].astype(BF16)
  wf = jnp.pad(w_in_fox[0, :, 3 * HD:], ((0, 0), (0, LANES - N_HEADS))).astype(BF16)
  wo = w_o_fox[0].astype(BF16)
  fox_args = (norm_mix[1], w, wf, b_forget[0], q_norm_g[0], k_norm_g[0])
  q, k, v, lf = _proj_fox(hp.reshape(b * t_len, d), *fox_args)
  fk_p, fv_p = heads(k, (b, t_len)), heads(v, (b, t_len))
  ff_p = lf.reshape(1, b, t_len, N_HEADS)
  o = _fox_attn_prompt(q.reshape(b, t_len, HD), k.reshape(b, t_len, HD), v.reshape(b, t_len, HD),
                       lf.reshape(b, t_len, N_HEADS))
  hp, cp, y_full = _post_prompt(hp, o, wo, norm_ffn[1], ffn[1], norm_final)
  conv_p.append(cp)

  q, k, v, lf = _proj_fox(hs, *fox_args)
  fk_s, fv_s = heads(k, (n, 1)), heads(v, (n, 1))
  ff_s = lf.reshape(1, n, 1, N_HEADS)
  o = _fox_attn_decode(q, k, v, lf, cache_fox_k[0], cache_fox_v[0], cache_fox_logf[0], page_table)
  hs, cs, y_s = _post_sample(hs, o, wo, norm_ffn[1], ffn[1], state_conv[1], norm_final)
  conv_s.append(cs)

  y_prompt = y_full[:, N_META:]
  y_sample = y_s.reshape(n, 1, d)
  return (y_prompt, y_sample, sbk_p, sbv_p, fk_p, fv_p, ff_p, jnp.stack(conv_p),
          sbk_s, sbv_s, fk_s, fv_s, ff_s, jnp.stack(conv_s))
```

```python
import functools

import jax
import jax.numpy as jnp
from jax import lax
from jax.experimental import pallas as pl
from jax.experimental.pallas import tpu as pltpu

F32 = jnp.float32
BF16 = jnp.bfloat16

N_HEADS = 16
HEAD_DIM = 64
HD = N_HEADS * HEAD_DIM
N_META = 16
CONV_WIDTH = 3
RMS_EPS = 1e-6
ATTN_SCALE = HEAD_DIM ** -0.5

LANES = 128
SUBLANES = 8
BF16_ROWS = 16
MXU_DIM = 256
BLK = LANES
QBLK = 2 * BLK
PAIR = 2 * HEAD_DIM
V7X_SCOPED_VMEM_BYTES = 60000 * 1024
NEG_BIG = -1e30

ROW_TILE_CAP = 688
FFN_CHUNK = MXU_DIM
DECODE_PAGES_PER_STEP = 8


def _row_tile(m, cap=ROW_TILE_CAP):
  best = None
  for t in range(BF16_ROWS, min(m, cap) + 1, BF16_ROWS):
    if m % t == 0:
      best = t
  assert best is not None, m
  return best


def _params(n_axes, vmem_bytes):
  assert vmem_bytes <= V7X_SCOPED_VMEM_BYTES
  return pltpu.CompilerParams(
      dimension_semantics=("arbitrary",) * n_axes, vmem_limit_bytes=int(vmem_bytes))


def _resident(shape):
  zeros = (0,) * len(shape)
  return pl.BlockSpec(shape, lambda *_: zeros, pipeline_mode=pl.Buffered(1))


def _rms(x, g):
  ms = jnp.mean(x * x, axis=-1, keepdims=True)
  return x * lax.rsqrt(ms + RMS_EPS) * g


def _neg_softplus(z):
  return -(jnp.maximum(z, 0.0) + jnp.log(1.0 + jnp.exp(-jnp.abs(z))))


def _split2(x):
  hi = x.astype(BF16)
  lo = (x - hi.astype(F32)).astype(BF16)
  return hi, lo


def _split3(x):
  hi = x.astype(BF16)
  r = x - hi.astype(F32)
  mid = r.astype(BF16)
  lo = (r - mid.astype(F32)).astype(BF16)
  return hi, mid, lo


def _dot(a, b):
  return jnp.dot(a, b, preferred_element_type=F32)


def _dot_nt(a, b):
  return lax.dot_general(a, b, (((1,), (1,)), ((), ())), preferred_element_type=F32)


def _dot_tn(a, b):
  return lax.dot_general(a, b, (((0,), (0,)), ((), ())), preferred_element_type=F32)


def _proj_sb_kernel(x_ref, g_ref, w_ref, q_ref, k_ref, v_ref):
  xn = _rms(x_ref[...], g_ref[...]).astype(BF16)
  q_ref[...] = (_dot(xn, w_ref[:, 0:HD]) * ATTN_SCALE).astype(BF16)
  k_ref[...] = _dot(xn, w_ref[:, HD:2 * HD])
  v_ref[...] = _dot(xn, w_ref[:, 2 * HD:3 * HD])


def _proj_sb(x2d, g, w):
  m, d = x2d.shape
  tm = _row_tile(m)
  row = lambda i: (i, 0)
  vmem = 2 * tm * (d * 4 + HD * 2 + 2 * HD * 4) + d * 3 * HD * 2 + 6 * tm * HD * 4
  return pl.pallas_call(
      _proj_sb_kernel,
      grid=(m // tm,),
      in_specs=[pl.BlockSpec((tm, d), row), _resident((1, d)), _resident((d, 3 * HD))],
      out_specs=[pl.BlockSpec((tm, HD), row)] * 3,
      out_shape=[jax.ShapeDtypeStruct((m, HD), BF16),
                 jax.ShapeDtypeStruct((m, HD), F32),
                 jax.ShapeDtypeStruct((m, HD), F32)],
      compiler_params=_params(1, vmem + (4 << 20)),
      name="proj_sb",
  )(x2d, g.reshape(1, d), w)


def _proj_fox_kernel(x_ref, g_ref, w_ref, wf_ref, bf_ref, gq_ref, gk_ref, s_ref,
                     q_ref, k_ref, v_ref, lf_ref):
  xn = _rms(x_ref[...], g_ref[...]).astype(BF16)

  def head_norm(y, gain):
    y2 = (y * y).astype(BF16)
    ss = jnp.concatenate(
        [_dot(y2[:, c * MXU_DIM:(c + 1) * MXU_DIM], s_ref[...]) for c in range(HD // MXU_DIM)],
        axis=1)
    return y * lax.rsqrt(ss * (1.0 / HEAD_DIM) + RMS_EPS) * gain

  q = head_norm(_dot(xn, w_ref[:, 0:HD]), gq_ref[...])
  q_ref[...] = (q * ATTN_SCALE).astype(BF16)
  k_ref[...] = head_norm(_dot(xn, w_ref[:, HD:2 * HD]), gk_ref[...])
  v_ref[...] = _dot(xn, w_ref[:, 2 * HD:3 * HD])
  f = _dot(xn, wf_ref[...])[:, 0:N_HEADS] + bf_ref[...]
  lf_ref[...] = jnp.minimum(f, 0.0) - jnp.log(1.0 + jnp.exp(-jnp.abs(f)))


def _proj_fox(x2d, g, w, wf, bf, gq, gk):
  m, d = x2d.shape
  tm = _row_tile(m)
  row = lambda i: (i, 0)
  r = jnp.arange(MXU_DIM) // HEAD_DIM
  seg = (r[:, None] == r[None, :]).astype(BF16)
  vmem = 2 * tm * (d * 4 + HD * 2 + 2 * HD * 4 + LANES * 4) + d * (3 * HD + LANES) * 2 + 8 * tm * HD * 4
  return pl.pallas_call(
      _proj_fox_kernel,
      grid=(m // tm,),
      in_specs=[pl.BlockSpec((tm, d), row), _resident((1, d)), _resident((d, 3 * HD)),
                _resident((d, LANES)), _resident((1, N_HEADS)), _resident((1, HD)),
                _resident((1, HD)), _resident((MXU_DIM, MXU_DIM))],
      out_specs=[pl.BlockSpec((tm, HD), row)] * 3 + [pl.BlockSpec((tm, N_HEADS), row)],
      out_shape=[jax.ShapeDtypeStruct((m, HD), BF16),
                 jax.ShapeDtypeStruct((m, HD), F32),
                 jax.ShapeDtypeStruct((m, HD), F32),
                 jax.ShapeDtypeStruct((m, N_HEADS), F32)],
      compiler_params=_params(1, min(vmem + (4 << 20), V7X_SCOPED_VMEM_BYTES)),
      name="proj_fox",
  )(x2d, g.reshape(1, d), w, wf, bf.reshape(1, N_HEADS),
    jnp.tile(gq, N_HEADS).reshape(1, HD), jnp.tile(gk, N_HEADS).reshape(1, HD), seg)


def _attn_geometry(t_len):
  nq, tail = divmod(t_len, QBLK)
  assert tail % BF16_ROWS == 0 and tail <= BLK
  return nq, tail, 2 * nq + (1 if tail else 0)


def _stage_kv(k_ref, v_ref, kab, vab, t_len):
  nq, tail, _ = _attn_geometry(t_len)

  def put(dst, d0, rows, x):
    head0 = lax.broadcasted_iota(jnp.int32, (rows, PAIR), 1) < HEAD_DIM
    dst[pl.ds(d0, rows), :] = jnp.where(head0, x, 0.0).astype(BF16)
    dst[pl.ds(pl.multiple_of(d0 + BLK, BLK), rows), :] = jnp.where(head0, 0.0, x).astype(BF16)

  def body(j, c):
    r0 = pl.multiple_of(j * BLK, BLK)
    d0 = pl.multiple_of(j * 2 * BLK, BLK)
    put(kab, d0, BLK, k_ref[0, pl.ds(r0, BLK), :])
    put(vab, d0, BLK, v_ref[0, pl.ds(r0, BLK), :])
    return c

  lax.fori_loop(0, 2 * nq, body, 0)
  if tail:
    d0 = 2 * nq * 2 * BLK
    kab[d0:d0 + 2 * BLK, :] = jnp.zeros((2 * BLK, PAIR), BF16)
    vab[d0:d0 + 2 * BLK, :] = jnp.zeros((2 * BLK, PAIR), BF16)
    put(kab, d0, tail, k_ref[0, nq * QBLK:t_len, :])
    put(vab, d0, tail, v_ref[0, nq * QBLK:t_len, :])


def _for_each_qblock(t_len, fn):
  nq, tail, _ = _attn_geometry(t_len)

  def body(i, c):
    fn(pl.multiple_of(i * QBLK, QBLK), QBLK, 2 * i, 2, i)
    return c

  lax.fori_loop(0, nq, body, 0)
  if tail:
    fn(nq * QBLK, tail, 2 * nq, 1, int(nq))


def _visit_earlier_keys(kv_step, jband, n_pairs):
  if isinstance(n_pairs, int):
    if n_pairs:
      kv_step(0, 2 * n_pairs, False)
    return

  def body(t, c):
    kv_step(jband - 2 - 2 * t, 2, False)
    return c

  lax.fori_loop(0, n_pairs, body, 0)


def _band_masks(tq, nblk, inclusive):
  row = lax.broadcasted_iota(jnp.int32, (tq, BLK), 0)
  col = lax.broadcasted_iota(jnp.int32, (tq, BLK), 1)
  blocks = [(u * BLK + col <= row) if inclusive else (u * BLK + col < row) for u in range(nblk)]
  return blocks, jnp.concatenate([m for m in blocks for _ in range(2)], axis=1)


def _sb_attn_kernel(q_ref, k_ref, v_ref, uw_ref, o_ref, kab, vab, r_ref, acc_ref, *, t_len):
  _stage_kv(k_ref, v_ref, kab, vab, t_len)

  def qblock(q0, tq, jband, nband, n_pairs):
    qb = q_ref[0, pl.ds(q0, tq), :]
    acc_ref[0:tq, :] = jnp.zeros((tq, PAIR), F32)
    r_ref[0:tq, :] = jnp.zeros((tq, 2 * BLK), F32)

    def kv_step(j0, nblk, band):
      d0 = pl.multiple_of(j0 * 2 * BLK, 2 * BLK)
      z = _dot_nt(qb, kab[pl.ds(d0, nblk * 2 * BLK), :])
      l1m = _neg_softplus(z)
      if band:
        valid, valid_all = _band_masks(tq, nblk, inclusive=False)
        l1m = jnp.where(valid_all, l1m, 0.0)
      hi, lo = _split2(l1m)
      r_old = r_ref[0:tq, :]
      a_parts = [None] * (2 * nblk)
      for h in range(2):
        r_h = r_old[:, h * BLK:(h + 1) * BLK]
        for u in reversed(range(nblk)):
          sl = slice(u * 2 * BLK + h * BLK, u * 2 * BLK + (h + 1) * BLK)
          ct = _dot(jnp.concatenate([hi[:, sl], lo[:, sl]], axis=1), uw_ref[...])
          a = jnp.exp(z[:, sl] + ct[:, 0:BLK] + r_h)
          if band:
            a = jnp.where(valid[u], a, 0.0)
          a_parts[2 * u + h] = a
          r_h = r_h + ct[:, BLK:2 * BLK]
        r_ref[0:tq, h * BLK:(h + 1) * BLK] = r_h
      a_all = jnp.concatenate(a_parts, axis=1).astype(BF16)
      acc_ref[0:tq, :] += _dot(a_all, vab[pl.ds(d0, nblk * 2 * BLK), :])

    kv_step(jband, nband, True)
    _visit_earlier_keys(kv_step, jband, n_pairs)
    o_ref[0, pl.ds(q0, tq), :] = acc_ref[0:tq, :].astype(BF16)

  _for_each_qblock(t_len, qblock)


def _suffix_sum_weights():
  j = jnp.arange(BLK)
  u = (j[:, None] >= j[None, :]).astype(BF16)
  half = jnp.concatenate([u, jnp.ones((BLK, BLK), BF16)], axis=1)
  return jnp.concatenate([half, half], axis=0)


def _sb_attn_prompt(q, k, v):
  b, t_len, _ = q.shape
  nb = _attn_geometry(t_len)[2]
  blk = pl.BlockSpec((1, t_len, PAIR), lambda i, p: (i, 0, p))
  return pl.pallas_call(
      functools.partial(_sb_attn_kernel, t_len=t_len),
      grid=(b, HD // PAIR),
      in_specs=[blk, blk, blk, _resident((2 * BLK, 2 * BLK))],
      out_specs=blk,
      out_shape=jax.ShapeDtypeStruct((b, t_len, HD), BF16),
      scratch_shapes=[pltpu.VMEM((nb * 2 * BLK, PAIR), BF16),
                      pltpu.VMEM((nb * 2 * BLK, PAIR), BF16),
                      pltpu.VMEM((QBLK, 2 * BLK), F32),
                      pltpu.VMEM((QBLK, PAIR), F32)],
      compiler_params=_params(2, 40 << 20),
      name="sb_attn_prompt",
  )(q, k, v, _suffix_sum_weights())


def _fox_attn_kernel(q_ref, k_ref, v_ref, lf_ref, ltri_ref, o_ref,
                     kab, vab, cq, ckt, m_ref, l_ref, acc_ref, *, t_len):
  _stage_kv(k_ref, v_ref, kab, vab, t_len)
  nb = _attn_geometry(t_len)[2]

  carry = jnp.zeros((1, 2 * BLK), F32)
  for j in range(nb):
    rows = min(BLK, t_len - j * BLK)
    lf = lf_ref[0, 0, j * BLK:j * BLK + rows, :]
    x = jnp.concatenate([jnp.broadcast_to(lf[:, h:h + 1], (rows, BLK)) for h in range(2)], axis=1)
    if rows < BLK:
      x = jnp.concatenate([x, jnp.zeros((BLK - rows, 2 * BLK), F32)], axis=0)
    c = carry
    for part in _split3(x):
      c = c + _dot(ltri_ref[...], part)
    cq[j * BLK:(j + 1) * BLK, :] = c
    carry = c[BLK - 1:BLK, :]
  for j in range(nb):
    cb = cq[j * BLK:(j + 1) * BLK, :]
    ckt[j] = jnp.concatenate([cb[:, 0:BLK].T[0:SUBLANES], cb[:, BLK:2 * BLK].T[0:SUBLANES]], axis=1)

  def qblock(q0, tq, jband, nband, n_pairs):
    head0 = lax.broadcasted_iota(jnp.int32, (tq, PAIR), 1) < HEAD_DIM
    qb = q_ref[0, pl.ds(q0, tq), :]
    cqb = cq[pl.ds(q0, tq), :]
    acc_ref[0:tq, :] = jnp.zeros((tq, PAIR), F32)
    m_ref[0:tq, :] = jnp.full((tq, 2 * BLK), NEG_BIG, F32)
    l_ref[0:tq, :] = jnp.zeros((tq, 2 * BLK), F32)

    def kv_step(j0, nblk, band):
      d0 = pl.multiple_of(j0 * 2 * BLK, 2 * BLK)
      ck = jnp.concatenate([ckt[j0 + u][0:1, :] for u in range(nblk)], axis=1)
      s = _dot_nt(qb, kab[pl.ds(d0, nblk * 2 * BLK), :]) + jnp.concatenate([cqb] * nblk, axis=1) - ck
      if band:
        s = jnp.where(_band_masks(tq, nblk, inclusive=True)[1], s, NEG_BIG)
      m_old = m_ref[0:tq, :]
      l_old = l_ref[0:tq, :]
      p_parts = [None] * (2 * nblk)
      alphas = []
      for h in range(2):
        sl = slice(h * BLK, (h + 1) * BLK)
        cols = [slice(u * 2 * BLK + h * BLK, u * 2 * BLK + (h + 1) * BLK) for u in range(nblk)]
        s_h = jnp.concatenate([s[:, c] for c in cols], axis=1)
        m_new = jnp.maximum(m_old[:, sl], jnp.max(s_h, axis=1, keepdims=True))
        p_acc = None
        for u in range(nblk):
          p = jnp.exp(s[:, cols[u]] - m_new)
          p_parts[2 * u + h] = p
          p_acc = p if p_acc is None else p_acc + p
        alpha = jnp.exp(m_old[:, sl] - m_new)
        m_ref[0:tq, sl] = m_new
        l_ref[0:tq, sl] = alpha * l_old[:, sl] + jnp.sum(p_acc, axis=1, keepdims=True)
        alphas.append(alpha)
      p_all = jnp.concatenate(p_parts, axis=1).astype(BF16)
      scale = jnp.where(head0, alphas[0], alphas[1])
      acc_ref[0:tq, :] = acc_ref[0:tq, :] * scale + _dot(p_all, vab[pl.ds(d0, nblk * 2 * BLK), :])

    kv_step(jband, nband, True)
    _visit_earlier_keys(kv_step, jband, n_pairs)
    l_fin = l_ref[0:tq, :]
    denom = jnp.where(head0, l_fin[:, 0:BLK], l_fin[:, BLK:2 * BLK])
    o_ref[0, pl.ds(q0, tq), :] = (acc_ref[0:tq, :] / denom).astype(BF16)

  _for_each_qblock(t_len, qblock)


def _fox_attn_prompt(q, k, v, logf):
  b, t_len, _ = q.shape
  nb = _attn_geometry(t_len)[2]
  n_pairs = HD // PAIR
  lf = logf.reshape(b, t_len, n_pairs, 2).transpose(0, 2, 1, 3)
  j = jnp.arange(BLK)
  ltri = (j[:, None] >= j[None, :]).astype(BF16)
  blk = pl.BlockSpec((1, t_len, PAIR), lambda i, p: (i, 0, p))
  return pl.pallas_call(
      functools.partial(_fox_attn_kernel, t_len=t_len),
      grid=(b, n_pairs),
      in_specs=[blk, blk, blk,
                pl.BlockSpec((1, 1, t_len, 2), lambda i, p: (i, p, 0, 0)),
                _resident((BLK, BLK))],
      out_specs=blk,
      out_shape=jax.ShapeDtypeStruct((b, t_len, HD), BF16),
      scratch_shapes=[pltpu.VMEM((nb * 2 * BLK, PAIR), BF16),
                      pltpu.VMEM((nb * 2 * BLK, PAIR), BF16),
                      pltpu.VMEM((nb * BLK, 2 * BLK), F32),
                      pltpu.VMEM((nb, SUBLANES, 2 * BLK), F32),
                      pltpu.VMEM((QBLK, 2 * BLK), F32),
                      pltpu.VMEM((QBLK, 2 * BLK), F32),
                      pltpu.VMEM((QBLK, PAIR), F32)],
      compiler_params=_params(2, 48 << 20),
      name="fox_attn_prompt",
  )(q, k, v, lf, ltri)


def _silu(x):
  return x * (1.0 / (1.0 + jnp.exp(-x)))


def _post_prompt_kernel(h_ref, o_ref, wo_ref, g_ref, wg_ref, wv_ref, cw_ref, cb_ref, wd_ref,
                        *rest, final):
  if final:
    gfin_ref, hout_ref, conv_ref, y_ref, gbuf, carry = rest
  else:
    hout_ref, conv_ref, gbuf, carry = rest
  t = pl.program_id(1)
  tm = h_ref.shape[1]
  n_chunks = wg_ref.shape[0]
  pad = SUBLANES

  @pl.when(t == 0)
  def _():
    carry[...] = jnp.zeros(carry.shape, F32)

  h1 = h_ref[0] + _dot(o_ref[0], wo_ref[...])
  hn = _rms(h1, g_ref[...]).astype(BF16)
  hout_ref[0] = h1

  def chunk(c, _):
    g = _dot(hn, wg_ref[c])
    val = _dot(hn, wv_ref[c])
    gbuf[0:pad, :] = carry[c]
    gbuf[pad:pad + tm, :] = g
    w = cw_ref[c]
    gc = (cb_ref[c] + w[0:1] * gbuf[pad - 2:pad - 2 + tm, :]
          + w[1:2] * gbuf[pad - 1:pad - 1 + tm, :] + w[2:3] * g)
    act = (_silu(gc) * val).astype(BF16)
    hout_ref[0] += _dot(act, wd_ref[c])
    last = gbuf[tm:tm + pad, :]
    carry[c] = last
    conv_ref[0, c] = last[pad - (CONV_WIDTH - 1):pad, :]
    return _

  lax.fori_loop(0, n_chunks, chunk, 0)
  if final:
    y_ref[0] = _rms(hout_ref[0], gfin_ref[...])


def _ffn_weights(w_up, conv_w, conv_b, w_down):
  d, two_ff = w_up.shape
  d_ff = two_ff // 2
  assert d_ff % FFN_CHUNK == 0
  nc = d_ff // FFN_CHUNK
  wg = w_up[:, :d_ff].astype(BF16).reshape(d, nc, FFN_CHUNK).transpose(1, 0, 2)
  wv = w_up[:, d_ff:].astype(BF16).reshape(d, nc, FFN_CHUNK).transpose(1, 0, 2)
  wd = w_down.astype(BF16).reshape(nc, FFN_CHUNK, d)
  cw = conv_w.reshape(CONV_WIDTH, nc, FFN_CHUNK).transpose(1, 0, 2)
  cb = conv_b.reshape(nc, 1, FFN_CHUNK)
  return wg, wv, cw, cb, wd


def _post_prompt(h, o, wo, g, ffn, g_final=None):
  b, t_len, d = h.shape
  wg, wv, cw, cb, wd = ffn
  nc = wg.shape[0]
  final = g_final is not None
  tm = _row_tile(t_len)
  tile = lambda i, t: (i, t, 0)
  act_spec = pl.BlockSpec((1, tm, d), tile)
  act_shape = jax.ShapeDtypeStruct((b, t_len, d), F32)
  w_bytes = 2 * (wo.size + wg.size + wv.size + wd.size)
  n_act = 3 if final else 2
  vmem = w_bytes + 2 * tm * d * (4 * n_act + 2) + 4 * tm * d * 4
  outs = pl.pallas_call(
      functools.partial(_post_prompt_kernel, final=final),
      grid=(b, t_len // tm),
      in_specs=[act_spec, pl.BlockSpec((1, tm, HD), tile), _resident(wo.shape), _resident((1, d)),
                _resident(wg.shape), _resident(wv.shape), _resident(cw.shape), _resident(cb.shape),
                _resident(wd.shape)] + ([_resident((1, d))] if final else []),
      out_specs=[act_spec,
                 pl.BlockSpec((1, nc, CONV_WIDTH - 1, FFN_CHUNK), lambda i, t: (i, 0, 0, 0))]
      + ([act_spec] if final else []),
      out_shape=[act_shape, jax.ShapeDtypeStruct((b, nc, CONV_WIDTH - 1, FFN_CHUNK), F32)]
      + ([act_shape] if final else []),
      scratch_shapes=[pltpu.VMEM((tm + SUBLANES, FFN_CHUNK), F32),
                      pltpu.VMEM((nc, SUBLANES, FFN_CHUNK), F32)],
      compiler_params=_params(2, min(vmem, V7X_SCOPED_VMEM_BYTES)),
      name="post_prompt_final" if final else "post_prompt",
  )(h, o, wo, g.reshape(1, d), wg, wv, cw, cb, wd, *([g_final.reshape(1, d)] if final else []))
  conv = outs[1].transpose(0, 2, 1, 3).reshape(b, CONV_WIDTH - 1, nc * FFN_CHUNK)
  return (outs[0], conv) + tuple(outs[2:])


def _post_sample_kernel(h_ref, o_ref, wo_ref, g_ref, wg_ref, wv_ref, cw_ref, cb_ref, wd_ref,
                        prev_ref, *rest, final):
  if final:
    gfin_ref, hout_ref, conv_ref, y_ref = rest
  else:
    hout_ref, conv_ref = rest
  n_chunks = wg_ref.shape[0]
  h1 = h_ref[...] + _dot(o_ref[...], wo_ref[...])
  hn = _rms(h1, g_ref[...]).astype(BF16)
  acc = h1
  for c in range(n_chunks):
    g = _dot(hn, wg_ref[c])
    val = _dot(hn, wv_ref[c])
    w = cw_ref[c]
    prev0 = prev_ref[0, c]
    prev1 = prev_ref[1, c]
    gc = cb_ref[c] + w[0:1] * prev0 + w[1:2] * prev1 + w[2:3] * g
    acc = acc + _dot((_silu(gc) * val).astype(BF16), wd_ref[c])
    conv_ref[0, c] = prev1
    conv_ref[1, c] = g
  hout_ref[...] = acc
  if final:
    y_ref[...] = _rms(acc, gfin_ref[...])


def _post_sample(h, o, wo, g, ffn, state, g_final=None):
  n, d = h.shape
  wg, wv, cw, cb, wd = ffn
  nc = wg.shape[0]
  final = g_final is not None
  prev = state.reshape(n, CONV_WIDTH - 1, nc, FFN_CHUNK).transpose(1, 2, 0, 3)
  full = lambda shape: pl.BlockSpec(shape, lambda i: (0,) * len(shape))
  row_shape = jax.ShapeDtypeStruct((n, d), F32)
  w_bytes = 2 * (wo.size + wg.size + wv.size + wd.size)
  outs = pl.pallas_call(
      functools.partial(_post_sample_kernel, final=final),
      grid=(1,),
      in_specs=[full((n, d)), full((n, HD)), _resident(wo.shape), full((1, d)),
                _resident(wg.shape), _resident(wv.shape), full(cw.shape), full(cb.shape),
                _resident(wd.shape), full(prev.shape)] + ([full((1, d))] if final else []),
      out_specs=[full((n, d)), full(prev.shape)] + ([full((n, d))] if final else []),
      out_shape=[row_shape, jax.ShapeDtypeStruct(prev.shape, F32)] + ([row_shape] if final else []),
      compiler_params=_params(1, min(w_bytes + (12 << 20), V7X_SCOPED_VMEM_BYTES)),
      name="post_sample_final" if final else "post_sample",
  )(h, o, wo, g.reshape(1, d), wg, wv, cw, cb, wd, prev, *([g_final.reshape(1, d)] if final else []))
  conv = outs[1].transpose(2, 0, 1, 3).reshape(n, CONV_WIDTH - 1, nc * FFN_CHUNK)
  return (outs[0], conv) + tuple(outs[2:])


def _lane_replicated(rows):
  return _dot_tn(rows, jnp.ones((BF16_ROWS, LANES), BF16)).reshape(N_HEADS, HEAD_DIM, LANES)


def _row0(x):
  row = lax.broadcasted_iota(jnp.int32, (BF16_ROWS, HD), 0)
  return jnp.where(row == 0, jnp.broadcast_to(x.astype(F32), (BF16_ROWS, HD)), 0.0).astype(BF16)


def _rows_split3(x):
  row = lax.broadcasted_iota(jnp.int32, (BF16_ROWS, HD), 0)
  out = jnp.zeros((BF16_ROWS, HD), F32)
  for i, part in enumerate(_split3(x)):
    out = jnp.where(row == i, jnp.broadcast_to(part.astype(F32), (BF16_ROWS, HD)), out)
  return out.astype(BF16)


def _over_dims(x):
  return x[:, None, :]


def _page_scores(qrep, k_refs):
  rows = [[] for _ in k_refs]
  for h in range(N_HEADS):
    qh = qrep[h]
    for i, k_ref in enumerate(k_refs):
      rows[i].append(jnp.sum(qh * k_ref[0, h], axis=0, keepdims=True))
  return [jnp.concatenate(r, axis=0) for r in rows]


def _accumulate_values(acc_ref, weights, v_refs, scale=None):
  for h in range(N_HEADS):
    upd = acc_ref[h]
    if scale is not None:
      upd = upd * scale[h:h + 1, :]
    for w, v_ref in zip(weights, v_refs):
      upd = upd + w[h:h + 1, :] * v_ref[0, h]
    acc_ref[h] = upd


def _sb_decode_kernel(pt_ref, q_ref, uw_ref, *refs, pages):
  k_refs = refs[0:pages]
  v_refs = refs[pages:2 * pages]
  o_ref, qrep, r_ref, acc_ref = refs[2 * pages:]
  g = pl.program_id(1)

  @pl.when(g == 0)
  def _():
    qrep[...] = _lane_replicated(_row0(q_ref[0]))
    r_ref[...] = jnp.zeros(r_ref.shape, F32)
    acc_ref[...] = jnp.zeros(acc_ref.shape, F32)

  weights = []
  r = r_ref[...]
  for z in _page_scores(qrep, k_refs):
    hi, lo = _split2(_neg_softplus(z))
    ct = _dot(jnp.concatenate([hi, lo], axis=1), uw_ref[...])
    weights.append(jnp.exp(z + ct[:, 0:BLK] + r))
    r = r + ct[:, BLK:2 * BLK]
  r_ref[...] = r
  _accumulate_values(acc_ref, weights, v_refs)

  @pl.when(g == pl.num_programs(1) - 1)
  def _():
    o_ref[0] = jnp.sum(acc_ref[...], axis=2, keepdims=True)


def _page_specs(n_pages, pages, block):
  def spec(i):
    return pl.BlockSpec(
        (1,) + block,
        lambda b, g, pt: (pt[b, n_pages - 1 - (g * pages + i)],) + (0,) * len(block))
  return [spec(i) for i in range(pages)]


def _physical_pages(pool):
  return jnp.transpose(pool, (0, 2, 3, 1))


def _sb_attn_decode(q, k_pool, v_pool, page_table):
  n = q.shape[0]
  page_size = k_pool.shape[1]
  assert page_size == BLK
  n_pages = page_table.shape[1]
  pages = DECODE_PAGES_PER_STEP
  assert n_pages % pages == 0
  page_block = (N_HEADS, HEAD_DIM, page_size)
  page_bytes = N_HEADS * HEAD_DIM * page_size * 4
  grid_spec = pltpu.PrefetchScalarGridSpec(
      num_scalar_prefetch=1,
      grid=(n, n_pages // pages),
      in_specs=[pl.BlockSpec((1, 1, HD), lambda b, g, pt: (b, 0, 0)),
                pl.BlockSpec((2 * BLK, 2 * BLK), lambda b, g, pt: (0, 0))]
      + _page_specs(n_pages, pages, page_block) * 2,
      out_specs=pl.BlockSpec((1, N_HEADS, HEAD_DIM, 1), lambda b, g, pt: (b, 0, 0, 0)),
      scratch_shapes=[pltpu.VMEM(page_block, F32),
                      pltpu.VMEM((N_HEADS, BLK), F32),
                      pltpu.VMEM(page_block, F32)],
  )
  o = pl.pallas_call(
      functools.partial(_sb_decode_kernel, pages=pages),
      grid_spec=grid_spec,
      out_shape=jax.ShapeDtypeStruct((n, N_HEADS, HEAD_DIM, 1), F32),
      compiler_params=_params(2, 2 * 2 * pages * page_bytes + (16 << 20)),
      name="sb_attn_decode",
  )(page_table, q.reshape(n, 1, HD), _suffix_sum_weights(),
    *([_physical_pages(k_pool)] * pages), *([_physical_pages(v_pool)] * pages))
  return o.reshape(n, HD).astype(BF16)


def _fox_decode_kernel(pt_ref, q_ref, kn_ref, vn_ref, lfn_ref, us_ref, *refs, pages):
  k_refs = refs[0:pages]
  v_refs = refs[pages:2 * pages]
  f_refs = refs[2 * pages:3 * pages]
  o_ref, qrep, d_ref, m_ref, l_ref, acc_ref = refs[3 * pages:]
  g = pl.program_id(1)

  @pl.when(g == 0)
  def _():
    qr = _lane_replicated(_row0(q_ref[0]))
    qrep[...] = qr
    m_ref[...] = jnp.sum(qr * _lane_replicated(_rows_split3(kn_ref[0])), axis=1)
    l_ref[...] = jnp.ones(l_ref.shape, F32)
    lane = lax.broadcasted_iota(jnp.int32, acc_ref.shape, 2)
    acc_ref[...] = jnp.where(lane == 0, _lane_replicated(_rows_split3(vn_ref[0])), 0.0)
    d_ref[...] = jnp.broadcast_to(lfn_ref[0], d_ref.shape)

  scores = []
  d = d_ref[...]
  for z, f_ref in zip(_page_scores(qrep, k_refs), f_refs):
    dt = _dot(jnp.concatenate(_split3(f_ref[0]), axis=1), us_ref[...])
    scores.append(z + dt[:, 0:BLK] + d)
    d = d + dt[:, BLK:2 * BLK]
  d_ref[...] = d
  m_old = m_ref[...]
  s_max = functools.reduce(jnp.maximum, scores)
  m_new = jnp.maximum(m_old, jnp.max(s_max, axis=1, keepdims=True))
  probs = [jnp.exp(s - m_new) for s in scores]
  alpha = jnp.exp(m_old - m_new)
  m_ref[...] = m_new
  l_ref[...] = alpha * l_ref[...] + jnp.sum(functools.reduce(jnp.add, probs), axis=1, keepdims=True)
  _accumulate_values(acc_ref, probs, v_refs, scale=alpha)

  @pl.when(g == pl.num_programs(1) - 1)
  def _():
    o_ref[0] = jnp.sum(acc_ref[...] * _over_dims(1.0 / l_ref[...]), axis=2, keepdims=True)


def _fox_attn_decode(q, k_new, v_new, lf_new, k_pool, v_pool, f_pool, page_table):
  n = q.shape[0]
  page_size = k_pool.shape[1]
  assert page_size == BLK
  n_pages = page_table.shape[1]
  pages = DECODE_PAGES_PER_STEP
  assert n_pages % pages == 0
  page_block = (N_HEADS, HEAD_DIM, page_size)
  page_bytes = N_HEADS * HEAD_DIM * page_size * 4
  j = jnp.arange(BLK)
  us = jnp.concatenate([(j[:, None] > j[None, :]).astype(BF16), jnp.ones((BLK, BLK), BF16)], axis=1)
  us = jnp.concatenate([us] * 3, axis=0)
  row = pl.BlockSpec((1, 1, HD), lambda b, g, pt: (b, 0, 0))
  grid_spec = pltpu.PrefetchScalarGridSpec(
      num_scalar_prefetch=1,
      grid=(n, n_pages // pages),
      in_specs=[row, row, row,
                pl.BlockSpec((1, N_HEADS, 1), lambda b, g, pt: (b, 0, 0)),
                pl.BlockSpec((3 * BLK, 2 * BLK), lambda b, g, pt: (0, 0))]
      + _page_specs(n_pages, pages, page_block) * 2
      + _page_specs(n_pages, pages, (N_HEADS, page_size)),
      out_specs=pl.BlockSpec((1, N_HEADS, HEAD_DIM, 1), lambda b, g, pt: (b, 0, 0, 0)),
      scratch_shapes=[pltpu.VMEM(page_block, F32),
                      pltpu.VMEM((N_HEADS, BLK), F32),
                      pltpu.VMEM((N_HEADS, BLK), F32),
                      pltpu.VMEM((N_HEADS, BLK), F32),
                      pltpu.VMEM(page_block, F32)],
  )
  o = pl.pallas_call(
      functools.partial(_fox_decode_kernel, pages=pages),
      grid_spec=grid_spec,
      out_shape=jax.ShapeDtypeStruct((n, N_HEADS, HEAD_DIM, 1), F32),
      compiler_params=_params(2, 2 * 2 * pages * page_bytes + (16 << 20)),
      name="fox_attn_decode",
  )(page_table, q.reshape(n, 1, HD), k_new.reshape(n, 1, HD), v_new.reshape(n, 1, HD),
    lf_new.reshape(n, N_HEADS, 1), us,
    *([_physical_pages(k_pool)] * pages), *([_physical_pages(v_pool)] * pages),
    *([jnp.transpose(f_pool, (0, 2, 1))] * pages))
  return o.reshape(n, HD).astype(BF16)


def kernel(x_prompt, x_sample, cache_sb_k, cache_sb_v, cache_fox_k, cache_fox_v, cache_fox_logf,
           state_conv, page_table, meta_tokens, norm_mix, norm_ffn, w_qkv_sb, w_o_sb, w_in_fox,
           b_forget, q_norm_g, k_norm_g, w_o_fox, w_up, conv_w, conv_b, w_down, norm_final):
  b, seq, d = x_prompt.shape
  n = x_sample.shape[0]
  assert x_sample.shape[1] == 1 and d == HD
  t_len = N_META + seq
  depth = norm_mix.shape[0]
  assert depth == 2 and w_qkv_sb.shape[0] == 1 and w_in_fox.shape[0] == 1

  meta = jnp.broadcast_to(meta_tokens.astype(x_prompt.dtype)[None], (b, N_META, d))
  hp = jnp.concatenate([meta, x_prompt], axis=1)
  hs = x_sample.reshape(n, d)

  heads = lambda a, lead: a.reshape(lead + (N_HEADS, HEAD_DIM))[None]
  ffn = [_ffn_weights(w_up[i], conv_w[i], conv_b[i], w_down[i]) for i in range(depth)]
  conv_p, conv_s = [], []

  w = w_qkv_sb[0].astype(BF16)
  wo = w_o_sb[0].astype(BF16)
  q, k, v = _proj_sb(hp.reshape(b * t_len, d), norm_mix[0], w)
  sbk_p, sbv_p = heads(k, (b, t_len)), heads(v, (b, t_len))
  o = _sb_attn_prompt(q.reshape(b, t_len, HD), k.reshape(b, t_len, HD), v.reshape(b, t_len, HD))
  hp, cp = _post_prompt(hp, o, wo, norm_ffn[0], ffn[0])
  conv_p.append(cp)

  q, k, v = _proj_sb(hs, norm_mix[0], w)
  sbk_s, sbv_s = heads(k, (n, 1)), heads(v, (n, 1))
  o = _sb_attn_decode(q, cache_sb_k[0], cache_sb_v[0], page_table)
  hs, cs = _post_sample(hs, o, wo, norm_ffn[0], ffn[0], state_conv[0])
  conv_s.append(cs)

  w = w_in_fox[0, :, :3 * HD].astype(BF16)
  wf = jnp.pad(w_in_fox[0, :, 3 * HD:], ((0, 0), (0, LANES - N_HEADS))).astype(BF16)
  wo = w_o_fox[0].astype(BF16)
  fox_args = (norm_mix[1], w, wf, b_forget[0], q_norm_g[0], k_norm_g[0])
  q, k, v, lf = _proj_fox(hp.reshape(b * t_len, d), *fox_args)
  fk_p, fv_p = heads(k, (b, t_len)), heads(v, (b, t_len))
  ff_p = lf.reshape(1, b, t_len, N_HEADS)
  o = _fox_attn_prompt(q.reshape(b, t_len, HD), k.reshape(b, t_len, HD), v.reshape(b, t_len, HD),
                       lf.reshape(b, t_len, N_HEADS))
  hp, cp, y_full = _post_prompt(hp, o, wo, norm_ffn[1], ffn[1], norm_final)
  conv_p.append(cp)

  q, k, v, lf = _proj_fox(hs, *fox_args)
  fk_s, fv_s = heads(k, (n, 1)), heads(v, (n, 1))
  ff_s = lf.reshape(1, n, 1, N_HEADS)
  o = _fox_attn_decode(q, k, v, lf, cache_fox_k[0], cache_fox_v[0], cache_fox_logf[0], page_table)
  hs, cs, y_s = _post_sample(hs, o, wo, norm_ffn[1], ffn[1], state_conv[1], norm_final)
  conv_s.append(cs)

  y_prompt = y_full[:, N_META:]
  y_sample = y_s.reshape(n, 1, d)
  return (y_prompt, y_sample, sbk_p, sbv_p, fk_p, fv_p, ff_p, jnp.stack(conv_p),
          sbk_s, sbv_s, fk_s, fv_s, ff_s, jnp.stack(conv_s))
```

```python
import functools

import jax
import jax.numpy as jnp
from jax import lax
from jax.experimental import pallas as pl
from jax.experimental.pallas import tpu as pltpu

F32 = jnp.float32
BF16 = jnp.bfloat16

N_HEADS = 16
HEAD_DIM = 64
HD = N_HEADS * HEAD_DIM
N_META = 16
CONV_WIDTH = 3
RMS_EPS = 1e-6
ATTN_SCALE = HEAD_DIM ** -0.5

LANES = 128
SUBLANES = 8
BF16_ROWS = 16
MXU_DIM = 256
BLK = LANES
QBLK = 2 * BLK
PAIR = 2 * HEAD_DIM
PAIRS_PER_STEP = 2
V7X_SCOPED_VMEM_BYTES = 60000 * 1024
NEG_BIG = -1e30

ROW_TILE_CAP = 688
FFN_CHUNK = MXU_DIM
DECODE_PAGES_PER_STEP = 8


def _row_tile(m, cap=ROW_TILE_CAP):
  best = None
  for t in range(BF16_ROWS, min(m, cap) + 1, BF16_ROWS):
    if m % t == 0:
      best = t
  assert best is not None, m
  return best


def _params(n_axes, vmem_bytes):
  assert vmem_bytes <= V7X_SCOPED_VMEM_BYTES
  return pltpu.CompilerParams(
      dimension_semantics=("arbitrary",) * n_axes, vmem_limit_bytes=int(vmem_bytes))


def _resident(shape):
  zeros = (0,) * len(shape)
  return pl.BlockSpec(shape, lambda *_: zeros, pipeline_mode=pl.Buffered(1))


def _rms(x, g):
  ms = jnp.mean(x * x, axis=-1, keepdims=True)
  return x * lax.rsqrt(ms + RMS_EPS) * g


def _neg_softplus(z):
  return -(jnp.maximum(z, 0.0) + jnp.log(1.0 + jnp.exp(-jnp.abs(z))))


def _split2(x):
  hi = x.astype(BF16)
  lo = (x - hi.astype(F32)).astype(BF16)
  return hi, lo


def _split3(x):
  hi = x.astype(BF16)
  r = x - hi.astype(F32)
  mid = r.astype(BF16)
  lo = (r - mid.astype(F32)).astype(BF16)
  return hi, mid, lo


def _dot(a, b):
  return jnp.dot(a, b, preferred_element_type=F32)


def _dot_nt(a, b):
  return lax.dot_general(a, b, (((1,), (1,)), ((), ())), preferred_element_type=F32)


def _dot_tn(a, b):
  return lax.dot_general(a, b, (((0,), (0,)), ((), ())), preferred_element_type=F32)


def _proj_sb_kernel(x_ref, g_ref, w_ref, q_ref, k_ref, v_ref):
  xn = _rms(x_ref[...], g_ref[...]).astype(BF16)
  q_ref[...] = (_dot(xn, w_ref[:, 0:HD]) * ATTN_SCALE).astype(BF16)
  k_ref[...] = _dot(xn, w_ref[:, HD:2 * HD])
  v_ref[...] = _dot(xn, w_ref[:, 2 * HD:3 * HD])


def _proj_sb(x2d, g, w):
  m, d = x2d.shape
  tm = _row_tile(m)
  row = lambda i: (i, 0)
  vmem = 2 * tm * (d * 4 + HD * 2 + 2 * HD * 4) + d * 3 * HD * 2 + 6 * tm * HD * 4
  return pl.pallas_call(
      _proj_sb_kernel,
      grid=(m // tm,),
      in_specs=[pl.BlockSpec((tm, d), row), _resident((1, d)), _resident((d, 3 * HD))],
      out_specs=[pl.BlockSpec((tm, HD), row)] * 3,
      out_shape=[jax.ShapeDtypeStruct((m, HD), BF16),
                 jax.ShapeDtypeStruct((m, HD), F32),
                 jax.ShapeDtypeStruct((m, HD), F32)],
      compiler_params=_params(1, vmem + (4 << 20)),
      name="proj_sb",
  )(x2d, g.reshape(1, d), w)


def _proj_fox_kernel(x_ref, g_ref, w_ref, wf_ref, bf_ref, gq_ref, gk_ref, s_ref,
                     q_ref, k_ref, v_ref, lf_ref):
  xn = _rms(x_ref[...], g_ref[...]).astype(BF16)

  def head_norm(y, gain):
    y2 = (y * y).astype(BF16)
    ss = jnp.concatenate(
        [_dot(y2[:, c * MXU_DIM:(c + 1) * MXU_DIM], s_ref[...]) for c in range(HD // MXU_DIM)],
        axis=1)
    return y * lax.rsqrt(ss * (1.0 / HEAD_DIM) + RMS_EPS) * gain

  q = head_norm(_dot(xn, w_ref[:, 0:HD]), gq_ref[...])
  q_ref[...] = (q * ATTN_SCALE).astype(BF16)
  k_ref[...] = head_norm(_dot(xn, w_ref[:, HD:2 * HD]), gk_ref[...])
  v_ref[...] = _dot(xn, w_ref[:, 2 * HD:3 * HD])
  f = _dot(xn, wf_ref[...])[:, 0:N_HEADS] + bf_ref[...]
  lf_ref[...] = jnp.minimum(f, 0.0) - jnp.log(1.0 + jnp.exp(-jnp.abs(f)))


def _proj_fox(x2d, g, w, wf, bf, gq, gk):
  m, d = x2d.shape
  tm = _row_tile(m)
  row = lambda i: (i, 0)
  r = jnp.arange(MXU_DIM) // HEAD_DIM
  seg = (r[:, None] == r[None, :]).astype(BF16)
  vmem = 2 * tm * (d * 4 + HD * 2 + 2 * HD * 4 + LANES * 4) + d * (3 * HD + LANES) * 2 + 8 * tm * HD * 4
  return pl.pallas_call(
      _proj_fox_kernel,
      grid=(m // tm,),
      in_specs=[pl.BlockSpec((tm, d), row), _resident((1, d)), _resident((d, 3 * HD)),
                _resident((d, LANES)), _resident((1, N_HEADS)), _resident((1, HD)),
                _resident((1, HD)), _resident((MXU_DIM, MXU_DIM))],
      out_specs=[pl.BlockSpec((tm, HD), row)] * 3 + [pl.BlockSpec((tm, N_HEADS), row)],
      out_shape=[jax.ShapeDtypeStruct((m, HD), BF16),
                 jax.ShapeDtypeStruct((m, HD), F32),
                 jax.ShapeDtypeStruct((m, HD), F32),
                 jax.ShapeDtypeStruct((m, N_HEADS), F32)],
      compiler_params=_params(1, min(vmem + (4 << 20), V7X_SCOPED_VMEM_BYTES)),
      name="proj_fox",
  )(x2d, g.reshape(1, d), w, wf, bf.reshape(1, N_HEADS),
    jnp.tile(gq, N_HEADS).reshape(1, HD), jnp.tile(gk, N_HEADS).reshape(1, HD), seg)


def _attn_geometry(t_len):
  nq, tail = divmod(t_len, QBLK)
  assert tail % BF16_ROWS == 0 and tail <= BLK
  return nq, tail, 2 * nq + (1 if tail else 0)


def _pair_lanes(n_par):
  return [slice(p * PAIR, (p + 1) * PAIR) for p in range(n_par)]


def _stage_kv(k_ref, v_ref, kt, vab, kout_ref, vout_ref, t_len, lanes):
  _, _, nb = _attn_geometry(t_len)
  head0 = lax.broadcasted_iota(jnp.int32, (BLK, PAIR), 1) < HEAD_DIM
  top = lax.broadcasted_iota(jnp.int32, (PAIR, BLK), 0) < HEAD_DIM
  for j in range(nb):
    rows = min(BLK, t_len - j * BLK)
    kb = k_ref[0, j * BLK:j * BLK + rows, lanes]
    vb = v_ref[0, j * BLK:j * BLK + rows, lanes]
    if rows < BLK:
      zeros = jnp.zeros((BLK - rows, PAIR), F32)
      kb = jnp.concatenate([kb, zeros], axis=0)
      vb = jnp.concatenate([vb, zeros], axis=0)
    kbt = kb.T
    kt[j] = jnp.concatenate([jnp.where(top, kbt, 0.0), jnp.where(top, 0.0, kbt)], axis=1).astype(BF16)
    vab[j * 2 * BLK:j * 2 * BLK + BLK, :] = jnp.where(head0, vb, 0.0).astype(BF16)
    vab[j * 2 * BLK + BLK:(j + 1) * 2 * BLK, :] = jnp.where(head0, 0.0, vb).astype(BF16)
    kout_ref[0, lanes, j * BLK:j * BLK + rows] = kbt[:, 0:rows]
    vout_ref[0, lanes, j * BLK:j * BLK + rows] = vb.T[:, 0:rows]


def _for_each_qblock(t_len, fn):
  nq, tail, _ = _attn_geometry(t_len)

  def body(i, c):
    fn(pl.multiple_of(i * QBLK, QBLK), QBLK, 2 * i, 2, i)
    return c

  lax.fori_loop(0, nq, body, 0)
  if tail:
    fn(nq * QBLK, tail, 2 * nq, 1, int(nq))


def _visit_earlier_keys(kv_step, jband, n_pairs):
  if isinstance(n_pairs, int):
    if n_pairs:
      kv_step(0, 2 * n_pairs, False)
    return

  def body(t, c):
    kv_step(jband - 2 - 2 * t, 2, False)
    return c

  lax.fori_loop(0, n_pairs, body, 0)


def _band_masks(tq, nblk, inclusive):
  row = lax.broadcasted_iota(jnp.int32, (tq, BLK), 0)
  col = lax.broadcasted_iota(jnp.int32, (tq, BLK), 1)
  blocks = [(u * BLK + col <= row) if inclusive else (u * BLK + col < row) for u in range(nblk)]
  return blocks, jnp.concatenate([m for m in blocks for _ in range(2)], axis=1)


def _pair_scores(qb, kt, j0, nblk):
  return jnp.concatenate([_dot(qb, kt[j0 + u]) for u in range(nblk)], axis=1)


def _sb_kv_step(qb, kt, vab, r_ref, acc_ref, uw_ref, tq, j0, nblk, band):
  z = _pair_scores(qb, kt, j0, nblk)
  l1m = _neg_softplus(z)
  if band:
    valid, valid_all = _band_masks(tq, nblk, inclusive=False)
    l1m = jnp.where(valid_all, l1m, 0.0)
  hi, lo = _split2(l1m)
  r_old = r_ref[0:tq, :]
  a_parts = [None] * (2 * nblk)
  for h in range(2):
    r_h = r_old[:, h * BLK:(h + 1) * BLK]
    for u in reversed(range(nblk)):
      sl = slice((2 * u + h) * BLK, (2 * u + h + 1) * BLK)
      ct = _dot(jnp.concatenate([hi[:, sl], lo[:, sl]], axis=1), uw_ref[...])
      a = jnp.exp(z[:, sl] + ct[:, 0:BLK] + r_h)
      if band:
        a = jnp.where(valid[u], a, 0.0)
      a_parts[2 * u + h] = a
      r_h = r_h + ct[:, BLK:2 * BLK]
    r_ref[0:tq, h * BLK:(h + 1) * BLK] = r_h
  a_all = jnp.concatenate(a_parts, axis=1).astype(BF16)
  d0 = pl.multiple_of(j0 * 2 * BLK, 2 * BLK)
  acc_ref[0:tq, :] += _dot(a_all, vab[pl.ds(d0, nblk * 2 * BLK), :])


def _sb_attn_kernel(q_ref, k_ref, v_ref, uw_ref, o_ref, kout_ref, vout_ref,
                    kt, vab, r_ref, acc_ref, *, t_len):
  n_par = kt.shape[0]
  lanes = _pair_lanes(n_par)
  for p in range(n_par):
    _stage_kv(k_ref, v_ref, kt.at[p], vab.at[p], kout_ref, vout_ref, t_len, lanes[p])

  def qblock(q0, tq, jband, nband, n_pairs):
    qbs = [q_ref[0, pl.ds(q0, tq), lanes[p]] for p in range(n_par)]
    for p in range(n_par):
      acc_ref[p, 0:tq, :] = jnp.zeros((tq, PAIR), F32)
      r_ref[p, 0:tq, :] = jnp.zeros((tq, 2 * BLK), F32)

    def kv_step(j0, nblk, band):
      for p in range(n_par):
        _sb_kv_step(qbs[p], kt.at[p], vab.at[p], r_ref.at[p], acc_ref.at[p], uw_ref,
                    tq, j0, nblk, band)

    kv_step(jband, nband, True)
    _visit_earlier_keys(kv_step, jband, n_pairs)
    for p in range(n_par):
      o_ref[0, pl.ds(q0, tq), lanes[p]] = acc_ref[p, 0:tq, :].astype(BF16)

  _for_each_qblock(t_len, qblock)


def _suffix_sum_weights():
  j = jnp.arange(BLK)
  u = (j[:, None] >= j[None, :]).astype(BF16)
  half = jnp.concatenate([u, jnp.ones((BLK, BLK), BF16)], axis=1)
  return jnp.concatenate([half, half], axis=0)


def _attn_specs(b, t_len):
  width = PAIRS_PER_STEP * PAIR
  blk = pl.BlockSpec((1, t_len, width), lambda i, p: (i, 0, p))
  blk_t = pl.BlockSpec((1, width, t_len), lambda i, p: (i, p, 0))
  out_specs = [blk, blk_t, blk_t]
  out_shape = [jax.ShapeDtypeStruct((b, t_len, HD), BF16),
               jax.ShapeDtypeStruct((b, HD, t_len), F32),
               jax.ShapeDtypeStruct((b, HD, t_len), F32)]
  return blk, out_specs, out_shape


def _cache_order(xt):
  b, _, t_len = xt.shape
  return jnp.transpose(xt.reshape(b, N_HEADS, HEAD_DIM, t_len), (0, 3, 1, 2))[None]


def _sb_attn_prompt(q, k, v):
  b, t_len, _ = q.shape
  nb = _attn_geometry(t_len)[2]
  n_par = PAIRS_PER_STEP
  blk, out_specs, out_shape = _attn_specs(b, t_len)
  return pl.pallas_call(
      functools.partial(_sb_attn_kernel, t_len=t_len),
      grid=(b, HD // (n_par * PAIR)),
      in_specs=[blk, blk, blk, _resident((2 * BLK, 2 * BLK))],
      out_specs=out_specs,
      out_shape=out_shape,
      scratch_shapes=[pltpu.VMEM((n_par, nb, PAIR, 2 * BLK), BF16),
                      pltpu.VMEM((n_par, nb * 2 * BLK, PAIR), BF16),
                      pltpu.VMEM((n_par, QBLK, 2 * BLK), F32),
                      pltpu.VMEM((n_par, QBLK, PAIR), F32)],
      compiler_params=_params(2, 48 << 20),
      name="sb_attn_prompt",
  )(q, k, v, _suffix_sum_weights())


def _stage_forget_bias(lf_ref, ltri_ref, cq, ckt, t_len):
  nb = _attn_geometry(t_len)[2]
  carry = jnp.zeros((1, 2 * BLK), F32)
  for j in range(nb):
    rows = min(BLK, t_len - j * BLK)
    lf = lf_ref[j * BLK:j * BLK + rows, :]
    x = jnp.concatenate([jnp.broadcast_to(lf[:, h:h + 1], (rows, BLK)) for h in range(2)], axis=1)
    if rows < BLK:
      x = jnp.concatenate([x, jnp.zeros((BLK - rows, 2 * BLK), F32)], axis=0)
    c = carry
    for part in _split3(x):
      c = c + _dot(ltri_ref[...], part)
    cq[j * BLK:(j + 1) * BLK, :] = c
    carry = c[BLK - 1:BLK, :]
  for j in range(nb):
    cb = cq[j * BLK:(j + 1) * BLK, :]
    ckt[j] = jnp.concatenate([cb[:, 0:BLK].T[0:SUBLANES], cb[:, BLK:2 * BLK].T[0:SUBLANES]], axis=1)


def _fox_kv_step(qb, cqb, kt, vab, ckt, m_ref, l_ref, acc_ref, tq, j0, nblk, band):
  head0 = lax.broadcasted_iota(jnp.int32, (tq, PAIR), 1) < HEAD_DIM
  d0 = pl.multiple_of(j0 * 2 * BLK, 2 * BLK)
  ck = jnp.concatenate([ckt[j0 + u][0:1, :] for u in range(nblk)], axis=1)
  s = _pair_scores(qb, kt, j0, nblk) + jnp.concatenate([cqb] * nblk, axis=1) - ck
  if band:
    s = jnp.where(_band_masks(tq, nblk, inclusive=True)[1], s, NEG_BIG)
  m_old = m_ref[0:tq, :]
  l_old = l_ref[0:tq, :]
  p_parts = [None] * (2 * nblk)
  alphas = []
  for h in range(2):
    sl = slice(h * BLK, (h + 1) * BLK)
    cols = [slice(u * 2 * BLK + h * BLK, u * 2 * BLK + (h + 1) * BLK) for u in range(nblk)]
    s_h = jnp.concatenate([s[:, c] for c in cols], axis=1)
    m_new = jnp.maximum(m_old[:, sl], jnp.max(s_h, axis=1, keepdims=True))
    p_acc = None
    for u in range(nblk):
      p = jnp.exp(s[:, cols[u]] - m_new)
      p_parts[2 * u + h] = p
      p_acc = p if p_acc is None else p_acc + p
    alpha = jnp.exp(m_old[:, sl] - m_new)
    m_ref[0:tq, sl] = m_new
    l_ref[0:tq, sl] = alpha * l_old[:, sl] + jnp.sum(p_acc, axis=1, keepdims=True)
    alphas.append(alpha)
  p_all = jnp.concatenate(p_parts, axis=1).astype(BF16)
  scale = jnp.where(head0, alphas[0], alphas[1])
  acc_ref[0:tq, :] = acc_ref[0:tq, :] * scale + _dot(p_all, vab[pl.ds(d0, nblk * 2 * BLK), :])


def _fox_attn_kernel(q_ref, k_ref, v_ref, lf_ref, ltri_ref, o_ref, kout_ref, vout_ref,
                     kt, vab, cq, ckt, m_ref, l_ref, acc_ref, *, t_len):
  n_par = kt.shape[0]
  lanes = _pair_lanes(n_par)
  for p in range(n_par):
    _stage_kv(k_ref, v_ref, kt.at[p], vab.at[p], kout_ref, vout_ref, t_len, lanes[p])
    _stage_forget_bias(lf_ref.at[0, p], ltri_ref, cq.at[p], ckt.at[p], t_len)

  def qblock(q0, tq, jband, nband, n_pairs):
    head0 = lax.broadcasted_iota(jnp.int32, (tq, PAIR), 1) < HEAD_DIM
    qbs = [q_ref[0, pl.ds(q0, tq), lanes[p]] for p in range(n_par)]
    cqbs = [cq[p, pl.ds(q0, tq), :] for p in range(n_par)]
    for p in range(n_par):
      acc_ref[p, 0:tq, :] = jnp.zeros((tq, PAIR), F32)
      m_ref[p, 0:tq, :] = jnp.full((tq, 2 * BLK), NEG_BIG, F32)
      l_ref[p, 0:tq, :] = jnp.zeros((tq, 2 * BLK), F32)

    def kv_step(j0, nblk, band):
      for p in range(n_par):
        _fox_kv_step(qbs[p], cqbs[p], kt.at[p], vab.at[p], ckt.at[p], m_ref.at[p], l_ref.at[p],
                     acc_ref.at[p], tq, j0, nblk, band)

    kv_step(jband, nband, True)
    _visit_earlier_keys(kv_step, jband, n_pairs)
    for p in range(n_par):
      l_fin = l_ref[p, 0:tq, :]
      denom = jnp.where(head0, l_fin[:, 0:BLK], l_fin[:, BLK:2 * BLK])
      o_ref[0, pl.ds(q0, tq), lanes[p]] = (acc_ref[p, 0:tq, :] / denom).astype(BF16)

  _for_each_qblock(t_len, qblock)


def _fox_attn_prompt(q, k, v, logf):
  b, t_len, _ = q.shape
  nb = _attn_geometry(t_len)[2]
  n_pairs = HD // PAIR
  lf = logf.reshape(b, t_len, n_pairs, 2).transpose(0, 2, 1, 3)
  j = jnp.arange(BLK)
  ltri = (j[:, None] >= j[None, :]).astype(BF16)
  n_par = PAIRS_PER_STEP
  blk, out_specs, out_shape = _attn_specs(b, t_len)
  return pl.pallas_call(
      functools.partial(_fox_attn_kernel, t_len=t_len),
      grid=(b, n_pairs // n_par),
      in_specs=[blk, blk, blk,
                pl.BlockSpec((1, n_par, t_len, 2), lambda i, p: (i, p, 0, 0)),
                _resident((BLK, BLK))],
      out_specs=out_specs,
      out_shape=out_shape,
      scratch_shapes=[pltpu.VMEM((n_par, nb, PAIR, 2 * BLK), BF16),
                      pltpu.VMEM((n_par, nb * 2 * BLK, PAIR), BF16),
                      pltpu.VMEM((n_par, nb * BLK, 2 * BLK), F32),
                      pltpu.VMEM((n_par, nb, SUBLANES, 2 * BLK), F32),
                      pltpu.VMEM((n_par, QBLK, 2 * BLK), F32),
                      pltpu.VMEM((n_par, QBLK, 2 * BLK), F32),
                      pltpu.VMEM((n_par, QBLK, PAIR), F32)],
      compiler_params=_params(2, 56 << 20),
      name="fox_attn_prompt",
  )(q, k, v, lf, ltri)


def _silu(x):
  return x * (1.0 / (1.0 + jnp.exp(-x)))


def _post_prompt_kernel(h_ref, o_ref, wo_ref, g_ref, wg_ref, wv_ref, cw_ref, cb_ref, wd_ref,
                        *rest, final):
  if final:
    gfin_ref, hout_ref, conv_ref, y_ref, gbuf, carry = rest
  else:
    hout_ref, conv_ref, gbuf, carry = rest
  t = pl.program_id(1)
  tm = h_ref.shape[1]
  n_chunks = wg_ref.shape[0]
  pad = SUBLANES

  @pl.when(t == 0)
  def _():
    carry[...] = jnp.zeros(carry.shape, F32)

  h1 = h_ref[0] + _dot(o_ref[0], wo_ref[...])
  hn = _rms(h1, g_ref[...]).astype(BF16)
  hout_ref[0] = h1

  def chunk(c, _):
    g = _dot(hn, wg_ref[c])
    val = _dot(hn, wv_ref[c])
    gbuf[0:pad, :] = carry[c]
    gbuf[pad:pad + tm, :] = g
    w = cw_ref[c]
    gc = (cb_ref[c] + w[0:1] * gbuf[pad - 2:pad - 2 + tm, :]
          + w[1:2] * gbuf[pad - 1:pad - 1 + tm, :] + w[2:3] * g)
    act = (_silu(gc) * val).astype(BF16)
    hout_ref[0] += _dot(act, wd_ref[c])
    last = gbuf[tm:tm + pad, :]
    carry[c] = last
    conv_ref[0, c] = last[pad - (CONV_WIDTH - 1):pad, :]
    return _

  lax.fori_loop(0, n_chunks, chunk, 0)
  if final:
    y_ref[0] = _rms(hout_ref[0], gfin_ref[...])


def _ffn_weights(w_up, conv_w, conv_b, w_down):
  d, two_ff = w_up.shape
  d_ff = two_ff // 2
  assert d_ff % FFN_CHUNK == 0
  nc = d_ff // FFN_CHUNK
  wg = w_up[:, :d_ff].astype(BF16).reshape(d, nc, FFN_CHUNK).transpose(1, 0, 2)
  wv = w_up[:, d_ff:].astype(BF16).reshape(d, nc, FFN_CHUNK).transpose(1, 0, 2)
  wd = w_down.astype(BF16).reshape(nc, FFN_CHUNK, d)
  cw = conv_w.reshape(CONV_WIDTH, nc, FFN_CHUNK).transpose(1, 0, 2)
  cb = conv_b.reshape(nc, 1, FFN_CHUNK)
  return wg, wv, cw, cb, wd


def _post_prompt(h, o, wo, g, ffn, g_final=None):
  b, t_len, d = h.shape
  wg, wv, cw, cb, wd = ffn
  nc = wg.shape[0]
  final = g_final is not None
  tm = _row_tile(t_len)
  tile = lambda i, t: (i, t, 0)
  act_spec = pl.BlockSpec((1, tm, d), tile)
  act_shape = jax.ShapeDtypeStruct((b, t_len, d), F32)
  w_bytes = 2 * (wo.size + wg.size + wv.size + wd.size)
  n_act = 3 if final else 2
  vmem = w_bytes + 2 * tm * d * (4 * n_act + 2) + 4 * tm * d * 4
  outs = pl.pallas_call(
      functools.partial(_post_prompt_kernel, final=final),
      grid=(b, t_len // tm),
      in_specs=[act_spec, pl.BlockSpec((1, tm, HD), tile), _resident(wo.shape), _resident((1, d)),
                _resident(wg.shape), _resident(wv.shape), _resident(cw.shape), _resident(cb.shape),
                _resident(wd.shape)] + ([_resident((1, d))] if final else []),
      out_specs=[act_spec,
                 pl.BlockSpec((1, nc, CONV_WIDTH - 1, FFN_CHUNK), lambda i, t: (i, 0, 0, 0))]
      + ([act_spec] if final else []),
      out_shape=[act_shape, jax.ShapeDtypeStruct((b, nc, CONV_WIDTH - 1, FFN_CHUNK), F32)]
      + ([act_shape] if final else []),
      scratch_shapes=[pltpu.VMEM((tm + SUBLANES, FFN_CHUNK), F32),
                      pltpu.VMEM((nc, SUBLANES, FFN_CHUNK), F32)],
      compiler_params=_params(2, min(vmem, V7X_SCOPED_VMEM_BYTES)),
      name="post_prompt_final" if final else "post_prompt",
  )(h, o, wo, g.reshape(1, d), wg, wv, cw, cb, wd, *([g_final.reshape(1, d)] if final else []))
  conv = outs[1].transpose(0, 2, 1, 3).reshape(b, CONV_WIDTH - 1, nc * FFN_CHUNK)
  return (outs[0], conv) + tuple(outs[2:])


def _post_sample_kernel(h_ref, o_ref, wo_ref, g_ref, wg_ref, wv_ref, cw_ref, cb_ref, wd_ref,
                        prev_ref, *rest, final):
  if final:
    gfin_ref, hout_ref, conv_ref, y_ref = rest
  else:
    hout_ref, conv_ref = rest
  n_chunks = wg_ref.shape[0]
  h1 = h_ref[...] + _dot(o_ref[...], wo_ref[...])
  hn = _rms(h1, g_ref[...]).astype(BF16)
  acc = h1
  for c in range(n_chunks):
    g = _dot(hn, wg_ref[c])
    val = _dot(hn, wv_ref[c])
    w = cw_ref[c]
    prev0 = prev_ref[0, c]
    prev1 = prev_ref[1, c]
    gc = cb_ref[c] + w[0:1] * prev0 + w[1:2] * prev1 + w[2:3] * g
    acc = acc + _dot((_silu(gc) * val).astype(BF16), wd_ref[c])
    conv_ref[0, c] = prev1
    conv_ref[1, c] = g
  hout_ref[...] = acc
  if final:
    y_ref[...] = _rms(acc, gfin_ref[...])


def _post_sample(h, o, wo, g, ffn, state, g_final=None):
  n, d = h.shape
  wg, wv, cw, cb, wd = ffn
  nc = wg.shape[0]
  final = g_final is not None
  prev = state.reshape(n, CONV_WIDTH - 1, nc, FFN_CHUNK).transpose(1, 2, 0, 3)
  full = lambda shape: pl.BlockSpec(shape, lambda i: (0,) * len(shape))
  row_shape = jax.ShapeDtypeStruct((n, d), F32)
  w_bytes = 2 * (wo.size + wg.size + wv.size + wd.size)
  outs = pl.pallas_call(
      functools.partial(_post_sample_kernel, final=final),
      grid=(1,),
      in_specs=[full((n, d)), full((n, HD)), _resident(wo.shape), full((1, d)),
                _resident(wg.shape), _resident(wv.shape), full(cw.shape), full(cb.shape),
                _resident(wd.shape), full(prev.shape)] + ([full((1, d))] if final else []),
      out_specs=[full((n, d)), full(prev.shape)] + ([full((n, d))] if final else []),
      out_shape=[row_shape, jax.ShapeDtypeStruct(prev.shape, F32)] + ([row_shape] if final else []),
      compiler_params=_params(1, min(w_bytes + (12 << 20), V7X_SCOPED_VMEM_BYTES)),
      name="post_sample_final" if final else "post_sample",
  )(h, o, wo, g.reshape(1, d), wg, wv, cw, cb, wd, prev, *([g_final.reshape(1, d)] if final else []))
  conv = outs[1].transpose(2, 0, 1, 3).reshape(n, CONV_WIDTH - 1, nc * FFN_CHUNK)
  return (outs[0], conv) + tuple(outs[2:])


def _lane_replicated(rows):
  return _dot_tn(rows, jnp.ones((BF16_ROWS, LANES), BF16)).reshape(N_HEADS, HEAD_DIM, LANES)


def _row0(x):
  row = lax.broadcasted_iota(jnp.int32, (BF16_ROWS, HD), 0)
  return jnp.where(row == 0, jnp.broadcast_to(x.astype(F32), (BF16_ROWS, HD)), 0.0).astype(BF16)


def _rows_split3(x):
  row = lax.broadcasted_iota(jnp.int32, (BF16_ROWS, HD), 0)
  out = jnp.zeros((BF16_ROWS, HD), F32)
  for i, part in enumerate(_split3(x)):
    out = jnp.where(row == i, jnp.broadcast_to(part.astype(F32), (BF16_ROWS, HD)), out)
  return out.astype(BF16)


def _over_dims(x):
  return x[:, None, :]


def _page_scores(qrep, k_refs):
  rows = [[] for _ in k_refs]
  for h in range(N_HEADS):
    qh = qrep[h]
    for i, k_ref in enumerate(k_refs):
      rows[i].append(jnp.sum(qh * k_ref[0, h], axis=0, keepdims=True))
  return [jnp.concatenate(r, axis=0) for r in rows]


def _accumulate_values(acc_ref, weights, v_refs, scale=None):
  for h in range(N_HEADS):
    upd = acc_ref[h]
    if scale is not None:
      upd = upd * scale[h:h + 1, :]
    for w, v_ref in zip(weights, v_refs):
      upd = upd + w[h:h + 1, :] * v_ref[0, h]
    acc_ref[h] = upd


def _sb_decode_kernel(pt_ref, q_ref, uw_ref, *refs, pages):
  k_refs = refs[0:pages]
  v_refs = refs[pages:2 * pages]
  o_ref, qrep, r_ref, acc_ref = refs[2 * pages:]
  g = pl.program_id(1)

  @pl.when(g == 0)
  def _():
    qrep[...] = _lane_replicated(_row0(q_ref[0]))
    r_ref[...] = jnp.zeros(r_ref.shape, F32)
    acc_ref[...] = jnp.zeros(acc_ref.shape, F32)

  weights = []
  r = r_ref[...]
  for z in _page_scores(qrep, k_refs):
    hi, lo = _split2(_neg_softplus(z))
    ct = _dot(jnp.concatenate([hi, lo], axis=1), uw_ref[...])
    weights.append(jnp.exp(z + ct[:, 0:BLK] + r))
    r = r + ct[:, BLK:2 * BLK]
  r_ref[...] = r
  _accumulate_values(acc_ref, weights, v_refs)

  @pl.when(g == pl.num_programs(1) - 1)
  def _():
    o_ref[0] = jnp.sum(acc_ref[...], axis=2, keepdims=True)


def _page_specs(n_pages, pages, block):
  def spec(i):
    return pl.BlockSpec(
        (1,) + block,
        lambda b, g, pt: (pt[b, n_pages - 1 - (g * pages + i)],) + (0,) * len(block))
  return [spec(i) for i in range(pages)]


def _physical_pages(pool):
  return jnp.transpose(pool, (0, 2, 3, 1))


def _sb_attn_decode(q, k_pool, v_pool, page_table):
  n = q.shape[0]
  page_size = k_pool.shape[1]
  assert page_size == BLK
  n_pages = page_table.shape[1]
  pages = DECODE_PAGES_PER_STEP
  assert n_pages % pages == 0
  page_block = (N_HEADS, HEAD_DIM, page_size)
  page_bytes = N_HEADS * HEAD_DIM * page_size * 4
  grid_spec = pltpu.PrefetchScalarGridSpec(
      num_scalar_prefetch=1,
      grid=(n, n_pages // pages),
      in_specs=[pl.BlockSpec((1, 1, HD), lambda b, g, pt: (b, 0, 0)),
                pl.BlockSpec((2 * BLK, 2 * BLK), lambda b, g, pt: (0, 0))]
      + _page_specs(n_pages, pages, page_block) * 2,
      out_specs=pl.BlockSpec((1, N_HEADS, HEAD_DIM, 1), lambda b, g, pt: (b, 0, 0, 0)),
      scratch_shapes=[pltpu.VMEM(page_block, F32),
                      pltpu.VMEM((N_HEADS, BLK), F32),
                      pltpu.VMEM(page_block, F32)],
  )
  o = pl.pallas_call(
      functools.partial(_sb_decode_kernel, pages=pages),
      grid_spec=grid_spec,
      out_shape=jax.ShapeDtypeStruct((n, N_HEADS, HEAD_DIM, 1), F32),
      compiler_params=_params(2, 2 * 2 * pages * page_bytes + (16 << 20)),
      name="sb_attn_decode",
  )(page_table, q.reshape(n, 1, HD), _suffix_sum_weights(),
    *([_physical_pages(k_pool)] * pages), *([_physical_pages(v_pool)] * pages))
  return o.reshape(n, HD).astype(BF16)


def _fox_decode_kernel(pt_ref, q_ref, kn_ref, vn_ref, lfn_ref, us_ref, *refs, pages):
  k_refs = refs[0:pages]
  v_refs = refs[pages:2 * pages]
  f_refs = refs[2 * pages:3 * pages]
  o_ref, qrep, d_ref, m_ref, l_ref, acc_ref = refs[3 * pages:]
  g = pl.program_id(1)

  @pl.when(g == 0)
  def _():
    qr = _lane_replicated(_row0(q_ref[0]))
    qrep[...] = qr
    m_ref[...] = jnp.sum(qr * _lane_replicated(_rows_split3(kn_ref[0])), axis=1)
    l_ref[...] = jnp.ones(l_ref.shape, F32)
    lane = lax.broadcasted_iota(jnp.int32, acc_ref.shape, 2)
    acc_ref[...] = jnp.where(lane == 0, _lane_replicated(_rows_split3(vn_ref[0])), 0.0)
    d_ref[...] = jnp.broadcast_to(lfn_ref[0], d_ref.shape)

  scores = []
  d = d_ref[...]
  for z, f_ref in zip(_page_scores(qrep, k_refs), f_refs):
    dt = _dot(jnp.concatenate(_split3(f_ref[0]), axis=1), us_ref[...])
    scores.append(z + dt[:, 0:BLK] + d)
    d = d + dt[:, BLK:2 * BLK]
  d_ref[...] = d
  m_old = m_ref[...]
  s_max = functools.reduce(jnp.maximum, scores)
  m_new = jnp.maximum(m_old, jnp.max(s_max, axis=1, keepdims=True))
  probs = [jnp.exp(s - m_new) for s in scores]
  alpha = jnp.exp(m_old - m_new)
  m_ref[...] = m_new
  l_ref[...] = alpha * l_ref[...] + jnp.sum(functools.reduce(jnp.add, probs), axis=1, keepdims=True)
  _accumulate_values(acc_ref, probs, v_refs, scale=alpha)

  @pl.when(g == pl.num_programs(1) - 1)
  def _():
    o_ref[0] = jnp.sum(acc_ref[...] * _over_dims(1.0 / l_ref[...]), axis=2, keepdims=True)


def _fox_attn_decode(q, k_new, v_new, lf_new, k_pool, v_pool, f_pool, page_table):
  n = q.shape[0]
  page_size = k_pool.shape[1]
  assert page_size == BLK
  n_pages = page_table.shape[1]
  pages = DECODE_PAGES_PER_STEP
  assert n_pages % pages == 0
  page_block = (N_HEADS, HEAD_DIM, page_size)
  page_bytes = N_HEADS * HEAD_DIM * page_size * 4
  j = jnp.arange(BLK)
  us = jnp.concatenate([(j[:, None] > j[None, :]).astype(BF16), jnp.ones((BLK, BLK), BF16)], axis=1)
  us = jnp.concatenate([us] * 3, axis=0)
  row = pl.BlockSpec((1, 1, HD), lambda b, g, pt: (b, 0, 0))
  grid_spec = pltpu.PrefetchScalarGridSpec(
      num_scalar_prefetch=1,
      grid=(n, n_pages // pages),
      in_specs=[row, row, row,
                pl.BlockSpec((1, N_HEADS, 1), lambda b, g, pt: (b, 0, 0)),
                pl.BlockSpec((3 * BLK, 2 * BLK), lambda b, g, pt: (0, 0))]
      + _page_specs(n_pages, pages, page_block) * 2
      + _page_specs(n_pages, pages, (N_HEADS, page_size)),
      out_specs=pl.BlockSpec((1, N_HEADS, HEAD_DIM, 1), lambda b, g, pt: (b, 0, 0, 0)),
      scratch_shapes=[pltpu.VMEM(page_block, F32),
                      pltpu.VMEM((N_HEADS, BLK), F32),
                      pltpu.VMEM((N_HEADS, BLK), F32),
                      pltpu.VMEM((N_HEADS, BLK), F32),
                      pltpu.VMEM(page_block, F32)],
  )
  o = pl.pallas_call(
      functools.partial(_fox_decode_kernel, pages=pages),
      grid_spec=grid_spec,
      out_shape=jax.ShapeDtypeStruct((n, N_HEADS, HEAD_DIM, 1), F32),
      compiler_params=_params(2, 2 * 2 * pages * page_bytes + (16 << 20)),
      name="fox_attn_decode",
  )(page_table, q.reshape(n, 1, HD), k_new.reshape(n, 1, HD), v_new.reshape(n, 1, HD),
    lf_new.reshape(n, N_HEADS, 1), us,
    *([_physical_pages(k_pool)] * pages), *([_physical_pages(v_pool)] * pages),
    *([jnp.transpose(f_pool, (0, 2, 1))] * pages))
  return o.reshape(n, HD).astype(BF16)


def kernel(x_prompt, x_sample, cache_sb_k, cache_sb_v, cache_fox_k, cache_fox_v, cache_fox_logf,
           state_conv, page_table, meta_tokens, norm_mix, norm_ffn, w_qkv_sb, w_o_sb, w_in_fox,
           b_forget, q_norm_g, k_norm_g, w_o_fox, w_up, conv_w, conv_b, w_down, norm_final):
  b, seq, d = x_prompt.shape
  n = x_sample.shape[0]
  assert x_sample.shape[1] == 1 and d == HD
  t_len = N_META + seq
  depth = norm_mix.shape[0]
  assert depth == 2 and w_qkv_sb.shape[0] == 1 and w_in_fox.shape[0] == 1

  meta = jnp.broadcast_to(meta_tokens.astype(x_prompt.dtype)[None], (b, N_META, d))
  hp = jnp.concatenate([meta, x_prompt], axis=1)
  hs = x_sample.reshape(n, d)

  heads = lambda a, lead: a.reshape(lead + (N_HEADS, HEAD_DIM))[None]
  ffn = [_ffn_weights(w_up[i], conv_w[i], conv_b[i], w_down[i]) for i in range(depth)]
  conv_p, conv_s = [], []

  w = w_qkv_sb[0].astype(BF16)
  wo = w_o_sb[0].astype(BF16)
  q, k, v = _proj_sb(hp.reshape(b * t_len, d), norm_mix[0], w)
  o, kt, vt = _sb_attn_prompt(q.reshape(b, t_len, HD), k.reshape(b, t_len, HD),
                              v.reshape(b, t_len, HD))
  sbk_p, sbv_p = _cache_order(kt), _cache_order(vt)
  hp, cp = _post_prompt(hp, o, wo, norm_ffn[0], ffn[0])
  conv_p.append(cp)

  q, k, v = _proj_sb(hs, norm_mix[0], w)
  sbk_s, sbv_s = heads(k, (n, 1)), heads(v, (n, 1))
  o = _sb_attn_decode(q, cache_sb_k[0], cache_sb_v[0], page_table)
  hs, cs = _post_sample(hs, o, wo, norm_ffn[0], ffn[0], state_conv[0])
  conv_s.append(cs)

  w = w_in_fox[0, :, :3 * HD].astype(BF16)
  wf = jnp.pad(w_in_fox[0, :, 3 * HD:], ((0, 0), (0, LANES - N_HEADS))).astype(BF16)
  wo = w_o_fox[0].astype(BF16)
  fox_args = (norm_mix[1], w, wf, b_forget[0], q_norm_g[0], k_norm_g[0])
  q, k, v, lf = _proj_fox(hp.reshape(b * t_len, d), *fox_args)
  ff_p = lf.reshape(1, b, t_len, N_HEADS)
  o, kt, vt = _fox_attn_prompt(q.reshape(b, t_len, HD), k.reshape(b, t_len, HD),
                               v.reshape(b, t_len, HD), lf.reshape(b, t_len, N_HEADS))
  fk_p, fv_p = _cache_order(kt), _cache_order(vt)
  hp, cp, y_full = _post_prompt(hp, o, wo, norm_ffn[1], ffn[1], norm_final)
  conv_p.append(cp)

  q, k, v, lf = _proj_fox(hs, *fox_args)
  fk_s, fv_s = heads(k, (n, 1)), heads(v, (n, 1))
  ff_s = lf.reshape(1, n, 1, N_HEADS)
  o = _fox_attn_decode(q, k, v, lf, cache_fox_k[0], cache_fox_v[0], cache_fox_logf[0], page_table)
  hs, cs, y_s = _post_sample(hs, o, wo, norm_ffn[1], ffn[1], state_conv[1], norm_final)
  conv_s.append(cs)

  y_prompt = y_full[:, N_META:]
  y_sample = y_s.reshape(n, 1, d)
  return (y_prompt, y_sample, sbk_p, sbv_p, fk_p, fv_p, ff_p, jnp.stack(conv_p),
          sbk_s, sbv_s, fk_s, fv_s, ff_s, jnp.stack(conv_s))
```

```python
import functools

import jax
import jax.numpy as jnp
from jax import lax
from jax.experimental import pallas as pl
from jax.experimental.pallas import tpu as pltpu

F32 = jnp.float32
BF16 = jnp.bfloat16

N_HEADS = 16
HEAD_DIM = 64
HD = N_HEADS * HEAD_DIM
N_META = 16
CONV_WIDTH = 3
RMS_EPS = 1e-6
ATTN_SCALE = HEAD_DIM ** -0.5

LANES = 128
SUBLANES = 8
BF16_ROWS = 16
MXU_DIM = 256
BLK = LANES
QBLK = 2 * BLK
PAIR = 2 * HEAD_DIM
PAIRS_PER_STEP = 2
V7X_SCOPED_VMEM_BYTES = 60000 * 1024
NEG_BIG = -1e30

ROW_TILE_CAP = 688
FFN_CHUNK = MXU_DIM
DECODE_PAGES_PER_STEP = 8


def _row_tile(m, cap=ROW_TILE_CAP):
  best = None
  for t in range(BF16_ROWS, min(m, cap) + 1, BF16_ROWS):
    if m % t == 0:
      best = t
  assert best is not None, m
  return best


def _params(n_axes, vmem_bytes):
  assert vmem_bytes <= V7X_SCOPED_VMEM_BYTES
  return pltpu.CompilerParams(
      dimension_semantics=("arbitrary",) * n_axes, vmem_limit_bytes=int(vmem_bytes))


def _resident(shape):
  zeros = (0,) * len(shape)
  return pl.BlockSpec(shape, lambda *_: zeros, pipeline_mode=pl.Buffered(1))


def _rms(x, g):
  ms = jnp.mean(x * x, axis=-1, keepdims=True)
  return x * lax.rsqrt(ms + RMS_EPS) * g


def _neg_softplus(z):
  return -(jnp.maximum(z, 0.0) + jnp.log(1.0 + jnp.exp(-jnp.abs(z))))


def _split2(x):
  hi = x.astype(BF16)
  lo = (x - hi.astype(F32)).astype(BF16)
  return hi, lo


def _split3(x):
  hi = x.astype(BF16)
  r = x - hi.astype(F32)
  mid = r.astype(BF16)
  lo = (r - mid.astype(F32)).astype(BF16)
  return hi, mid, lo


def _dot(a, b):
  return jnp.dot(a, b, preferred_element_type=F32)


def _dot_nt(a, b):
  return lax.dot_general(a, b, (((1,), (1,)), ((), ())), preferred_element_type=F32)


def _dot_tn(a, b):
  return lax.dot_general(a, b, (((0,), (0,)), ((), ())), preferred_element_type=F32)


def _proj_sb_kernel(x_ref, g_ref, w_ref, q_ref, k_ref, v_ref):
  xn = _rms(x_ref[...], g_ref[...]).astype(BF16)
  q_ref[...] = (_dot(xn, w_ref[:, 0:HD]) * ATTN_SCALE).astype(BF16)
  k_ref[...] = _dot(xn, w_ref[:, HD:2 * HD])
  v_ref[...] = _dot(xn, w_ref[:, 2 * HD:3 * HD])


def _proj_sb(x2d, g, w):
  m, d = x2d.shape
  tm = _row_tile(m)
  row = lambda i: (i, 0)
  vmem = 2 * tm * (d * 4 + HD * 2 + 2 * HD * 4) + d * 3 * HD * 2 + 6 * tm * HD * 4
  return pl.pallas_call(
      _proj_sb_kernel,
      grid=(m // tm,),
      in_specs=[pl.BlockSpec((tm, d), row), _resident((1, d)), _resident((d, 3 * HD))],
      out_specs=[pl.BlockSpec((tm, HD), row)] * 3,
      out_shape=[jax.ShapeDtypeStruct((m, HD), BF16),
                 jax.ShapeDtypeStruct((m, HD), F32),
                 jax.ShapeDtypeStruct((m, HD), F32)],
      compiler_params=_params(1, vmem + (4 << 20)),
      name="proj_sb",
  )(x2d, g.reshape(1, d), w)


def _proj_fox_kernel(x_ref, g_ref, w_ref, wf_ref, bf_ref, gq_ref, gk_ref, s_ref,
                     q_ref, k_ref, v_ref, lf_ref):
  xn = _rms(x_ref[...], g_ref[...]).astype(BF16)

  def head_norm(y, gain):
    y2 = (y * y).astype(BF16)
    ss = jnp.concatenate(
        [_dot(y2[:, c * MXU_DIM:(c + 1) * MXU_DIM], s_ref[...]) for c in range(HD // MXU_DIM)],
        axis=1)
    return y * lax.rsqrt(ss * (1.0 / HEAD_DIM) + RMS_EPS) * gain

  q = head_norm(_dot(xn, w_ref[:, 0:HD]), gq_ref[...])
  q_ref[...] = (q * ATTN_SCALE).astype(BF16)
  k_ref[...] = head_norm(_dot(xn, w_ref[:, HD:2 * HD]), gk_ref[...])
  v_ref[...] = _dot(xn, w_ref[:, 2 * HD:3 * HD])
  f = _dot(xn, wf_ref[...])[:, 0:N_HEADS] + bf_ref[...]
  lf_ref[...] = jnp.minimum(f, 0.0) - jnp.log(1.0 + jnp.exp(-jnp.abs(f)))


def _proj_fox(x2d, g, w, wf, bf, gq, gk):
  m, d = x2d.shape
  tm = _row_tile(m)
  row = lambda i: (i, 0)
  r = jnp.arange(MXU_DIM) // HEAD_DIM
  seg = (r[:, None] == r[None, :]).astype(BF16)
  vmem = 2 * tm * (d * 4 + HD * 2 + 2 * HD * 4 + LANES * 4) + d * (3 * HD + LANES) * 2 + 8 * tm * HD * 4
  return pl.pallas_call(
      _proj_fox_kernel,
      grid=(m // tm,),
      in_specs=[pl.BlockSpec((tm, d), row), _resident((1, d)), _resident((d, 3 * HD)),
                _resident((d, LANES)), _resident((1, N_HEADS)), _resident((1, HD)),
                _resident((1, HD)), _resident((MXU_DIM, MXU_DIM))],
      out_specs=[pl.BlockSpec((tm, HD), row)] * 3 + [pl.BlockSpec((tm, N_HEADS), row)],
      out_shape=[jax.ShapeDtypeStruct((m, HD), BF16),
                 jax.ShapeDtypeStruct((m, HD), F32),
                 jax.ShapeDtypeStruct((m, HD), F32),
                 jax.ShapeDtypeStruct((m, N_HEADS), F32)],
      compiler_params=_params(1, min(vmem + (4 << 20), V7X_SCOPED_VMEM_BYTES)),
      name="proj_fox",
  )(x2d, g.reshape(1, d), w, wf, bf.reshape(1, N_HEADS),
    jnp.tile(gq, N_HEADS).reshape(1, HD), jnp.tile(gk, N_HEADS).reshape(1, HD), seg)


def _attn_geometry(t_len):
  nq, tail = divmod(t_len, QBLK)
  assert tail % BF16_ROWS == 0 and tail <= BLK
  return nq, tail, 2 * nq + (1 if tail else 0)


def _pair_lanes(n_par):
  return [slice(p * PAIR, (p + 1) * PAIR) for p in range(n_par)]


def _stage_kv(k_ref, v_ref, kt, vab, kout_ref, vout_ref, t_len, lanes):
  _, _, nb = _attn_geometry(t_len)
  head0 = lax.broadcasted_iota(jnp.int32, (BLK, PAIR), 1) < HEAD_DIM
  top = lax.broadcasted_iota(jnp.int32, (PAIR, BLK), 0) < HEAD_DIM
  for j in range(nb):
    rows = min(BLK, t_len - j * BLK)
    kb = k_ref[0, j * BLK:j * BLK + rows, lanes]
    vb = v_ref[0, j * BLK:j * BLK + rows, lanes]
    if rows < BLK:
      zeros = jnp.zeros((BLK - rows, PAIR), F32)
      kb = jnp.concatenate([kb, zeros], axis=0)
      vb = jnp.concatenate([vb, zeros], axis=0)
    kbt = kb.T
    kt[j] = jnp.concatenate([jnp.where(top, kbt, 0.0), jnp.where(top, 0.0, kbt)], axis=1).astype(BF16)
    vab[j * 2 * BLK:j * 2 * BLK + BLK, :] = jnp.where(head0, vb, 0.0).astype(BF16)
    vab[j * 2 * BLK + BLK:(j + 1) * 2 * BLK, :] = jnp.where(head0, 0.0, vb).astype(BF16)
    kout_ref[0, lanes, j * BLK:j * BLK + rows] = kbt[:, 0:rows]
    vout_ref[0, lanes, j * BLK:j * BLK + rows] = vb.T[:, 0:rows]


def _for_each_qblock(t_len, fn, hook=None):
  nq, tail, _ = _attn_geometry(t_len)

  def body(i, c):
    if hook is not None:
      hook(i)
    fn(pl.multiple_of(i * QBLK, QBLK), QBLK, 2 * i, 2, i)
    return c

  lax.fori_loop(0, nq, body, 0)
  if tail:
    fn(nq * QBLK, tail, 2 * nq, 1, int(nq))


def _visit_earlier_keys(kv_step, jband, n_pairs):
  if isinstance(n_pairs, int):
    if n_pairs:
      kv_step(0, 2 * n_pairs, False)
    return

  def body(t, c):
    kv_step(jband - 2 - 2 * t, 2, False)
    return c

  lax.fori_loop(0, n_pairs, body, 0)


def _band_masks(tq, nblk, inclusive):
  row = lax.broadcasted_iota(jnp.int32, (tq, BLK), 0)
  col = lax.broadcasted_iota(jnp.int32, (tq, BLK), 1)
  blocks = [(u * BLK + col <= row) if inclusive else (u * BLK + col < row) for u in range(nblk)]
  return blocks, jnp.concatenate([m for m in blocks for _ in range(2)], axis=1)


def _pair_scores(qb, kt, j0, nblk):
  return jnp.concatenate([_dot(qb, kt[j0 + u]) for u in range(nblk)], axis=1)


def _sb_kv_step(qb, kt, vab, r_ref, acc_ref, uw_ref, tq, j0, nblk, band):
  z = _pair_scores(qb, kt, j0, nblk)
  l1m = _neg_softplus(z)
  if band:
    valid, valid_all = _band_masks(tq, nblk, inclusive=False)
    l1m = jnp.where(valid_all, l1m, 0.0)
  hi, lo = _split2(l1m)
  r_old = r_ref[0:tq, :]
  a_parts = [None] * (2 * nblk)
  for h in range(2):
    r_h = r_old[:, h * BLK:(h + 1) * BLK]
    for u in reversed(range(nblk)):
      sl = slice((2 * u + h) * BLK, (2 * u + h + 1) * BLK)
      ct = _dot(jnp.concatenate([hi[:, sl], lo[:, sl]], axis=1), uw_ref[...])
      a = jnp.exp(z[:, sl] + ct[:, 0:BLK] + r_h)
      if band:
        a = jnp.where(valid[u], a, 0.0)
      a_parts[2 * u + h] = a
      r_h = r_h + ct[:, BLK:2 * BLK]
    r_ref[0:tq, h * BLK:(h + 1) * BLK] = r_h
  a_all = jnp.concatenate(a_parts, axis=1).astype(BF16)
  d0 = pl.multiple_of(j0 * 2 * BLK, 2 * BLK)
  acc_ref[0:tq, :] += _dot(a_all, vab[pl.ds(d0, nblk * 2 * BLK), :])


def _page_copies(pt_ref, seq, chunk, pools, bufs, sems, slot, n_pages, pages):
  copies = []
  for pg in range(pages):
    page = pt_ref[seq, n_pages - 1 - (chunk * pages + pg)]
    for t, (pool, buf) in enumerate(zip(pools, bufs)):
      copies.append(pltpu.make_async_copy(pool.at[page], buf.at[slot, pg], sems.at[slot, t]))
  return copies


def _sb_attn_kernel(pt_ref, q_ref, k_ref, v_ref, uw_ref, qd_ref, kpool, vpool,
                    o_ref, kout_ref, vout_ref, od_ref,
                    kt, vab, r_ref, acc_ref, qrep, rd_ref, accd_ref, kbuf, vbuf, sems,
                    *, t_len, n_pages, pages):
  seq = pl.program_id(0) * pl.num_programs(1) + pl.program_id(1)
  n_chunks = n_pages // pages
  copies = lambda chunk, slot: _page_copies(
      pt_ref, seq, chunk, (kpool, vpool), (kbuf, vbuf), sems, slot, n_pages, pages)
  for c in copies(0, 0):
    c.start()
  _sb_decode_init(qd_ref[0], qrep, rd_ref, accd_ref)

  def decode_chunk(i):
    slot = lax.rem(i, 2)

    @pl.when(i + 1 < n_chunks)
    def _():
      for c in copies(i + 1, 1 - slot):
        c.start()

    for c in copies(i, slot):
      c.wait()
    _sb_decode_pages(qrep, rd_ref, accd_ref, uw_ref,
                     [kbuf.at[slot, pg] for pg in range(pages)],
                     [vbuf.at[slot, pg] for pg in range(pages)])

  n_par = kt.shape[0]
  lanes = _pair_lanes(n_par)
  for p in range(n_par):
    _stage_kv(k_ref, v_ref, kt.at[p], vab.at[p], kout_ref, vout_ref, t_len, lanes[p])

  def qblock(q0, tq, jband, nband, n_pairs):
    qbs = [q_ref[0, pl.ds(q0, tq), lanes[p]] for p in range(n_par)]
    for p in range(n_par):
      acc_ref[p, 0:tq, :] = jnp.zeros((tq, PAIR), F32)
      r_ref[p, 0:tq, :] = jnp.zeros((tq, 2 * BLK), F32)

    def kv_step(j0, nblk, band):
      for p in range(n_par):
        _sb_kv_step(qbs[p], kt.at[p], vab.at[p], r_ref.at[p], acc_ref.at[p], uw_ref,
                    tq, j0, nblk, band)

    kv_step(jband, nband, True)
    _visit_earlier_keys(kv_step, jband, n_pairs)
    for p in range(n_par):
      o_ref[0, pl.ds(q0, tq), lanes[p]] = acc_ref[p, 0:tq, :].astype(BF16)

  _for_each_qblock(t_len, qblock, hook=decode_chunk)
  od_ref[0] = jnp.sum(accd_ref[...], axis=2, keepdims=True)


def _suffix_sum_weights():
  j = jnp.arange(BLK)
  u = (j[:, None] >= j[None, :]).astype(BF16)
  half = jnp.concatenate([u, jnp.ones((BLK, BLK), BF16)], axis=1)
  return jnp.concatenate([half, half], axis=0)


def _attn_specs(b, t_len):
  width = PAIRS_PER_STEP * PAIR
  blk = pl.BlockSpec((1, t_len, width), lambda i, p, *_: (i, 0, p))
  blk_t = pl.BlockSpec((1, width, t_len), lambda i, p, *_: (i, p, 0))
  out_specs = [blk, blk_t, blk_t]
  out_shape = [jax.ShapeDtypeStruct((b, t_len, HD), BF16),
               jax.ShapeDtypeStruct((b, HD, t_len), F32),
               jax.ShapeDtypeStruct((b, HD, t_len), F32)]
  return blk, out_specs, out_shape


def _cache_order(xt):
  b, _, t_len = xt.shape
  return jnp.transpose(xt.reshape(b, N_HEADS, HEAD_DIM, t_len), (0, 3, 1, 2))[None]


def _sb_attn(q, k, v, q_dec, k_pool, v_pool, page_table):
  b, t_len, _ = q.shape
  nq, _, nb = _attn_geometry(t_len)
  n_par = PAIRS_PER_STEP
  n_groups = HD // (n_par * PAIR)
  n = q_dec.shape[0]
  page_size = k_pool.shape[1]
  n_pages = page_table.shape[1]
  pages = DECODE_PAGES_PER_STEP
  assert page_size == BLK and n == b * n_groups and n_pages == nq * pages
  page_block = (N_HEADS, HEAD_DIM, page_size)
  page_bytes = N_HEADS * HEAD_DIM * page_size * 4
  blk, out_specs, out_shape = _attn_specs(b, t_len)
  seq = lambda i, p, pt: i * n_groups + p
  grid_spec = pltpu.PrefetchScalarGridSpec(
      num_scalar_prefetch=1,
      grid=(b, n_groups),
      in_specs=[blk, blk, blk, _resident((2 * BLK, 2 * BLK)),
                pl.BlockSpec((1, 1, HD), lambda i, p, pt: (seq(i, p, pt), 0, 0)),
                pl.BlockSpec(memory_space=pl.ANY), pl.BlockSpec(memory_space=pl.ANY)],
      out_specs=out_specs + [pl.BlockSpec((1, N_HEADS, HEAD_DIM, 1),
                                          lambda i, p, pt: (seq(i, p, pt), 0, 0, 0))],
      scratch_shapes=[pltpu.VMEM((n_par, nb, PAIR, 2 * BLK), BF16),
                      pltpu.VMEM((n_par, nb * 2 * BLK, PAIR), BF16),
                      pltpu.VMEM((n_par, QBLK, 2 * BLK), F32),
                      pltpu.VMEM((n_par, QBLK, PAIR), F32),
                      pltpu.VMEM(page_block, F32),
                      pltpu.VMEM((N_HEADS, BLK), F32),
                      pltpu.VMEM(page_block, F32),
                      pltpu.VMEM((2, pages) + page_block, F32),
                      pltpu.VMEM((2, pages) + page_block, F32),
                      pltpu.SemaphoreType.DMA((2, 2))],
  )
  o, kt, vt, o_dec = pl.pallas_call(
      functools.partial(_sb_attn_kernel, t_len=t_len, n_pages=n_pages, pages=pages),
      grid_spec=grid_spec,
      out_shape=out_shape + [jax.ShapeDtypeStruct((n, N_HEADS, HEAD_DIM, 1), F32)],
      compiler_params=_params(2, (36 << 20) + 2 * 2 * pages * page_bytes),
      name="sb_attn",
  )(page_table, q, k, v, _suffix_sum_weights(), q_dec.reshape(n, 1, HD),
    _physical_pages(k_pool), _physical_pages(v_pool))
  return o, kt, vt, o_dec.reshape(n, HD).astype(BF16)


def _stage_forget_bias(lf_ref, ltri_ref, cq, ckt, t_len):
  nb = _attn_geometry(t_len)[2]
  carry = jnp.zeros((1, 2 * BLK), F32)
  for j in range(nb):
    rows = min(BLK, t_len - j * BLK)
    lf = lf_ref[j * BLK:j * BLK + rows, :]
    x = jnp.concatenate([jnp.broadcast_to(lf[:, h:h + 1], (rows, BLK)) for h in range(2)], axis=1)
    if rows < BLK:
      x = jnp.concatenate([x, jnp.zeros((BLK - rows, 2 * BLK), F32)], axis=0)
    c = carry
    for part in _split3(x):
      c = c + _dot(ltri_ref[...], part)
    cq[j * BLK:(j + 1) * BLK, :] = c
    carry = c[BLK - 1:BLK, :]
  for j in range(nb):
    cb = cq[j * BLK:(j + 1) * BLK, :]
    ckt[j] = jnp.concatenate([cb[:, 0:BLK].T[0:SUBLANES], cb[:, BLK:2 * BLK].T[0:SUBLANES]], axis=1)


def _fox_kv_step(qb, cqb, kt, vab, ckt, m_ref, l_ref, acc_ref, tq, j0, nblk, band):
  head0 = lax.broadcasted_iota(jnp.int32, (tq, PAIR), 1) < HEAD_DIM
  d0 = pl.multiple_of(j0 * 2 * BLK, 2 * BLK)
  ck = jnp.concatenate([ckt[j0 + u][0:1, :] for u in range(nblk)], axis=1)
  s = _pair_scores(qb, kt, j0, nblk) + jnp.concatenate([cqb] * nblk, axis=1) - ck
  if band:
    s = jnp.where(_band_masks(tq, nblk, inclusive=True)[1], s, NEG_BIG)
  m_old = m_ref[0:tq, :]
  l_old = l_ref[0:tq, :]
  p_parts = [None] * (2 * nblk)
  alphas = []
  for h in range(2):
    sl = slice(h * BLK, (h + 1) * BLK)
    cols = [slice(u * 2 * BLK + h * BLK, u * 2 * BLK + (h + 1) * BLK) for u in range(nblk)]
    s_h = jnp.concatenate([s[:, c] for c in cols], axis=1)
    m_new = jnp.maximum(m_old[:, sl], jnp.max(s_h, axis=1, keepdims=True))
    p_acc = None
    for u in range(nblk):
      p = jnp.exp(s[:, cols[u]] - m_new)
      p_parts[2 * u + h] = p
      p_acc = p if p_acc is None else p_acc + p
    alpha = jnp.exp(m_old[:, sl] - m_new)
    m_ref[0:tq, sl] = m_new
    l_ref[0:tq, sl] = alpha * l_old[:, sl] + jnp.sum(p_acc, axis=1, keepdims=True)
    alphas.append(alpha)
  p_all = jnp.concatenate(p_parts, axis=1).astype(BF16)
  scale = jnp.where(head0, alphas[0], alphas[1])
  acc_ref[0:tq, :] = acc_ref[0:tq, :] * scale + _dot(p_all, vab[pl.ds(d0, nblk * 2 * BLK), :])


def _fox_attn_kernel(pt_ref, q_ref, k_ref, v_ref, lf_ref, ltri_ref,
                     qd_ref, knd_ref, vnd_ref, lfnd_ref, us_ref, kpool, vpool, fpool,
                     o_ref, kout_ref, vout_ref, od_ref,
                     kt, vab, cq, ckt, m_ref, l_ref, acc_ref,
                     qrep, dd_ref, md_ref, ld_ref, accd_ref, kbuf, vbuf, fbuf, sems,
                     *, t_len, n_pages, pages):
  seq = pl.program_id(0) * pl.num_programs(1) + pl.program_id(1)
  n_chunks = n_pages // pages
  copies = lambda chunk, slot: _page_copies(
      pt_ref, seq, chunk, (kpool, vpool, fpool), (kbuf, vbuf, fbuf), sems, slot, n_pages, pages)
  for c in copies(0, 0):
    c.start()
  _fox_decode_init(qd_ref[0], knd_ref[0], vnd_ref[0], lfnd_ref[0],
                   qrep, dd_ref, md_ref, ld_ref, accd_ref)

  def decode_chunk(i):
    slot = lax.rem(i, 2)

    @pl.when(i + 1 < n_chunks)
    def _():
      for c in copies(i + 1, 1 - slot):
        c.start()

    for c in copies(i, slot):
      c.wait()
    _fox_decode_pages(qrep, dd_ref, md_ref, ld_ref, accd_ref, us_ref,
                      [kbuf.at[slot, pg] for pg in range(pages)],
                      [vbuf.at[slot, pg] for pg in range(pages)],
                      [fbuf.at[slot, pg] for pg in range(pages)])

  n_par = kt.shape[0]
  lanes = _pair_lanes(n_par)
  for p in range(n_par):
    _stage_kv(k_ref, v_ref, kt.at[p], vab.at[p], kout_ref, vout_ref, t_len, lanes[p])
    _stage_forget_bias(lf_ref.at[0, p], ltri_ref, cq.at[p], ckt.at[p], t_len)

  def qblock(q0, tq, jband, nband, n_pairs):
    head0 = lax.broadcasted_iota(jnp.int32, (tq, PAIR), 1) < HEAD_DIM
    qbs = [q_ref[0, pl.ds(q0, tq), lanes[p]] for p in range(n_par)]
    cqbs = [cq[p, pl.ds(q0, tq), :] for p in range(n_par)]
    for p in range(n_par):
      acc_ref[p, 0:tq, :] = jnp.zeros((tq, PAIR), F32)
      m_ref[p, 0:tq, :] = jnp.full((tq, 2 * BLK), NEG_BIG, F32)
      l_ref[p, 0:tq, :] = jnp.zeros((tq, 2 * BLK), F32)

    def kv_step(j0, nblk, band):
      for p in range(n_par):
        _fox_kv_step(qbs[p], cqbs[p], kt.at[p], vab.at[p], ckt.at[p], m_ref.at[p], l_ref.at[p],
                     acc_ref.at[p], tq, j0, nblk, band)

    kv_step(jband, nband, True)
    _visit_earlier_keys(kv_step, jband, n_pairs)
    for p in range(n_par):
      l_fin = l_ref[p, 0:tq, :]
      denom = jnp.where(head0, l_fin[:, 0:BLK], l_fin[:, BLK:2 * BLK])
      o_ref[0, pl.ds(q0, tq), lanes[p]] = (acc_ref[p, 0:tq, :] / denom).astype(BF16)

  _for_each_qblock(t_len, qblock, hook=decode_chunk)
  od_ref[0] = _fox_decode_result(ld_ref, accd_ref)


def _fox_attn(q, k, v, logf, q_dec, k_new, v_new, lf_new, k_pool, v_pool, f_pool, page_table):
  b, t_len, _ = q.shape
  nq, _, nb = _attn_geometry(t_len)
  n_pairs = HD // PAIR
  n_par = PAIRS_PER_STEP
  n_groups = n_pairs // n_par
  n = q_dec.shape[0]
  page_size = k_pool.shape[1]
  n_pages = page_table.shape[1]
  pages = DECODE_PAGES_PER_STEP
  assert page_size == BLK and n == b * n_groups and n_pages == nq * pages
  page_block = (N_HEADS, HEAD_DIM, page_size)
  page_bytes = N_HEADS * HEAD_DIM * page_size * 4
  lf = logf.reshape(b, t_len, n_pairs, 2).transpose(0, 2, 1, 3)
  j = jnp.arange(BLK)
  ltri = (j[:, None] >= j[None, :]).astype(BF16)
  blk, out_specs, out_shape = _attn_specs(b, t_len)
  once = pl.BlockSpec(blk.block_shape, blk.index_map, pipeline_mode=pl.Buffered(1))
  seq = lambda i, p: i * n_groups + p
  row = pl.BlockSpec((1, 1, HD), lambda i, p, pt: (seq(i, p), 0, 0))
  hbm = pl.BlockSpec(memory_space=pl.ANY)
  grid_spec = pltpu.PrefetchScalarGridSpec(
      num_scalar_prefetch=1,
      grid=(b, n_groups),
      in_specs=[blk, once, once,
                pl.BlockSpec((1, n_par, t_len, 2), lambda i, p, pt: (i, p, 0, 0),
                             pipeline_mode=pl.Buffered(1)),
                _resident((BLK, BLK)),
                row, row, row,
                pl.BlockSpec((1, N_HEADS, 1), lambda i, p, pt: (seq(i, p), 0, 0)),
                _resident((3 * BLK, 2 * BLK)), hbm, hbm, hbm],
      out_specs=out_specs + [pl.BlockSpec((1, N_HEADS, HEAD_DIM, 1),
                                          lambda i, p, pt: (seq(i, p), 0, 0, 0))],
      scratch_shapes=[pltpu.VMEM((n_par, nb, PAIR, 2 * BLK), BF16),
                      pltpu.VMEM((n_par, nb * 2 * BLK, PAIR), BF16),
                      pltpu.VMEM((n_par, nb * BLK, 2 * BLK), F32),
                      pltpu.VMEM((n_par, nb, SUBLANES, 2 * BLK), F32),
                      pltpu.VMEM((n_par, QBLK, 2 * BLK), F32),
                      pltpu.VMEM((n_par, QBLK, 2 * BLK), F32),
                      pltpu.VMEM((n_par, QBLK, PAIR), F32),
                      pltpu.VMEM(page_block, F32),
                      pltpu.VMEM((N_HEADS, BLK), F32),
                      pltpu.VMEM((N_HEADS, BLK), F32),
                      pltpu.VMEM((N_HEADS, BLK), F32),
                      pltpu.VMEM(page_block, F32),
                      pltpu.VMEM((2, pages) + page_block, F32),
                      pltpu.VMEM((2, pages) + page_block, F32),
                      pltpu.VMEM((2, pages, N_HEADS, page_size), F32),
                      pltpu.SemaphoreType.DMA((2, 3))],
  )
  o, kt, vt, o_dec = pl.pallas_call(
      functools.partial(_fox_attn_kernel, t_len=t_len, n_pages=n_pages, pages=pages),
      grid_spec=grid_spec,
      out_shape=out_shape + [jax.ShapeDtypeStruct((n, N_HEADS, HEAD_DIM, 1), F32)],
      compiler_params=_params(2, min((42 << 20) + 2 * 2 * pages * page_bytes, V7X_SCOPED_VMEM_BYTES)),
      name="fox_attn",
  )(page_table, q, k, v, lf, ltri, q_dec.reshape(n, 1, HD), k_new.reshape(n, 1, HD),
    v_new.reshape(n, 1, HD), lf_new.reshape(n, N_HEADS, 1), _forget_suffix_weights(),
    _physical_pages(k_pool), _physical_pages(v_pool), jnp.transpose(f_pool, (0, 2, 1)))
  return o, kt, vt, o_dec.reshape(n, HD).astype(BF16)


def _silu(x):
  return x * (1.0 / (1.0 + jnp.exp(-x)))


def _post_prompt_kernel(h_ref, o_ref, wo_ref, g_ref, wg_ref, wv_ref, cw_ref, cb_ref, wd_ref,
                        *rest, final):
  if final:
    gfin_ref, hout_ref, conv_ref, y_ref, gbuf0, gbuf1, vbuf0, vbuf1, carry = rest
  else:
    hout_ref, conv_ref, gbuf0, gbuf1, vbuf0, vbuf1, carry = rest
  t = pl.program_id(1)
  tm = h_ref.shape[1]
  n_chunks = wg_ref.shape[0]
  pad = SUBLANES

  @pl.when(t == 0)
  def _():
    carry[...] = jnp.zeros(carry.shape, F32)

  h1 = h_ref[0] + _dot(o_ref[0], wo_ref[...])
  hn = _rms(h1, g_ref[...]).astype(BF16)
  hout_ref[0] = h1

  def up_proj(c, gbuf, vbuf):
    gbuf[0:pad, :] = carry[c]
    gbuf[pad:pad + tm, :] = _dot(hn, wg_ref[c])
    vbuf[...] = _dot(hn, wv_ref[c])

  def conv_down(c, gbuf, vbuf):
    w = cw_ref[c]
    gc = (cb_ref[c] + w[0:1] * gbuf[pad - 2:pad - 2 + tm, :]
          + w[1:2] * gbuf[pad - 1:pad - 1 + tm, :] + w[2:3] * gbuf[pad:pad + tm, :])
    act = (_silu(gc) * vbuf[...]).astype(BF16)
    hout_ref[0] += _dot(act, wd_ref[c])
    last = gbuf[tm:tm + pad, :]
    carry[c] = last
    conv_ref[0, c] = last[pad - (CONV_WIDTH - 1):pad, :]

  up_proj(0, gbuf0, vbuf0)

  def two_chunks(k, _):
    conv_down(2 * k, gbuf0, vbuf0)
    up_proj(2 * k + 1, gbuf1, vbuf1)
    conv_down(2 * k + 1, gbuf1, vbuf1)
    up_proj(2 * k + 2, gbuf0, vbuf0)
    return _

  lax.fori_loop(0, (n_chunks - 1) // 2, two_chunks, 0)
  if n_chunks % 2 == 0:
    conv_down(n_chunks - 2, gbuf0, vbuf0)
    up_proj(n_chunks - 1, gbuf1, vbuf1)
    conv_down(n_chunks - 1, gbuf1, vbuf1)
  else:
    conv_down(n_chunks - 1, gbuf0, vbuf0)
  if final:
    y_ref[0] = _rms(hout_ref[0], gfin_ref[...])


def _ffn_weights(w_up, conv_w, conv_b, w_down):
  d, two_ff = w_up.shape
  d_ff = two_ff // 2
  assert d_ff % FFN_CHUNK == 0
  nc = d_ff // FFN_CHUNK
  wg = w_up[:, :d_ff].astype(BF16).reshape(d, nc, FFN_CHUNK).transpose(1, 0, 2)
  wv = w_up[:, d_ff:].astype(BF16).reshape(d, nc, FFN_CHUNK).transpose(1, 0, 2)
  wd = w_down.astype(BF16).reshape(nc, FFN_CHUNK, d)
  cw = conv_w.reshape(CONV_WIDTH, nc, FFN_CHUNK).transpose(1, 0, 2)
  cb = conv_b.reshape(nc, 1, FFN_CHUNK)
  return wg, wv, cw, cb, wd


def _post_prompt(h, o, wo, g, ffn, g_final=None):
  b, t_len, d = h.shape
  wg, wv, cw, cb, wd = ffn
  nc = wg.shape[0]
  final = g_final is not None
  tm = _row_tile(t_len)
  tile = lambda i, t: (i, t, 0)
  act_spec = pl.BlockSpec((1, tm, d), tile)
  act_shape = jax.ShapeDtypeStruct((b, t_len, d), F32)
  w_bytes = 2 * (wo.size + wg.size + wv.size + wd.size)
  n_act = 3 if final else 2
  vmem = w_bytes + 2 * tm * d * (4 * n_act + 2) + 4 * tm * d * 4
  outs = pl.pallas_call(
      functools.partial(_post_prompt_kernel, final=final),
      grid=(b, t_len // tm),
      in_specs=[act_spec, pl.BlockSpec((1, tm, HD), tile), _resident(wo.shape), _resident((1, d)),
                _resident(wg.shape), _resident(wv.shape), _resident(cw.shape), _resident(cb.shape),
                _resident(wd.shape)] + ([_resident((1, d))] if final else []),
      out_specs=[act_spec,
                 pl.BlockSpec((1, nc, CONV_WIDTH - 1, FFN_CHUNK), lambda i, t: (i, 0, 0, 0))]
      + ([act_spec] if final else []),
      out_shape=[act_shape, jax.ShapeDtypeStruct((b, nc, CONV_WIDTH - 1, FFN_CHUNK), F32)]
      + ([act_shape] if final else []),
      scratch_shapes=[pltpu.VMEM((tm + SUBLANES, FFN_CHUNK), F32)] * 2
      + [pltpu.VMEM((tm, FFN_CHUNK), F32)] * 2
      + [pltpu.VMEM((nc, SUBLANES, FFN_CHUNK), F32)],
      compiler_params=_params(2, min(vmem, V7X_SCOPED_VMEM_BYTES)),
      name="post_prompt_final" if final else "post_prompt",
  )(h, o, wo, g.reshape(1, d), wg, wv, cw, cb, wd, *([g_final.reshape(1, d)] if final else []))
  conv = outs[1].transpose(0, 2, 1, 3).reshape(b, CONV_WIDTH - 1, nc * FFN_CHUNK)
  return (outs[0], conv) + tuple(outs[2:])


def _post_sample_kernel(h_ref, o_ref, wo_ref, g_ref, wg_ref, wv_ref, cw_ref, cb_ref, wd_ref,
                        prev_ref, *rest, final):
  if final:
    gfin_ref, hout_ref, conv_ref, y_ref = rest
  else:
    hout_ref, conv_ref = rest
  n_chunks = wg_ref.shape[0]
  h1 = h_ref[...] + _dot(o_ref[...], wo_ref[...])
  hn = _rms(h1, g_ref[...]).astype(BF16)
  acc = h1
  for c in range(n_chunks):
    g = _dot(hn, wg_ref[c])
    val = _dot(hn, wv_ref[c])
    w = cw_ref[c]
    prev0 = prev_ref[0, c]
    prev1 = prev_ref[1, c]
    gc = cb_ref[c] + w[0:1] * prev0 + w[1:2] * prev1 + w[2:3] * g
    acc = acc + _dot((_silu(gc) * val).astype(BF16), wd_ref[c])
    conv_ref[0, c] = prev1
    conv_ref[1, c] = g
  hout_ref[...] = acc
  if final:
    y_ref[...] = _rms(acc, gfin_ref[...])


def _post_sample(h, o, wo, g, ffn, state, g_final=None):
  n, d = h.shape
  wg, wv, cw, cb, wd = ffn
  nc = wg.shape[0]
  final = g_final is not None
  prev = state.reshape(n, CONV_WIDTH - 1, nc, FFN_CHUNK).transpose(1, 2, 0, 3)
  full = lambda shape: pl.BlockSpec(shape, lambda i: (0,) * len(shape))
  row_shape = jax.ShapeDtypeStruct((n, d), F32)
  w_bytes = 2 * (wo.size + wg.size + wv.size + wd.size)
  outs = pl.pallas_call(
      functools.partial(_post_sample_kernel, final=final),
      grid=(1,),
      in_specs=[full((n, d)), full((n, HD)), _resident(wo.shape), full((1, d)),
                _resident(wg.shape), _resident(wv.shape), full(cw.shape), full(cb.shape),
                _resident(wd.shape), full(prev.shape)] + ([full((1, d))] if final else []),
      out_specs=[full((n, d)), full(prev.shape)] + ([full((n, d))] if final else []),
      out_shape=[row_shape, jax.ShapeDtypeStruct(prev.shape, F32)] + ([row_shape] if final else []),
      compiler_params=_params(1, min(w_bytes + (12 << 20), V7X_SCOPED_VMEM_BYTES)),
      name="post_sample_final" if final else "post_sample",
  )(h, o, wo, g.reshape(1, d), wg, wv, cw, cb, wd, prev, *([g_final.reshape(1, d)] if final else []))
  conv = outs[1].transpose(2, 0, 1, 3).reshape(n, CONV_WIDTH - 1, nc * FFN_CHUNK)
  return (outs[0], conv) + tuple(outs[2:])


def _lane_replicated(rows):
  return _dot_tn(rows, jnp.ones((BF16_ROWS, LANES), BF16)).reshape(N_HEADS, HEAD_DIM, LANES)


def _row0(x):
  row = lax.broadcasted_iota(jnp.int32, (BF16_ROWS, HD), 0)
  return jnp.where(row == 0, jnp.broadcast_to(x.astype(F32), (BF16_ROWS, HD)), 0.0).astype(BF16)


def _rows_split3(x):
  row = lax.broadcasted_iota(jnp.int32, (BF16_ROWS, HD), 0)
  out = jnp.zeros((BF16_ROWS, HD), F32)
  for i, part in enumerate(_split3(x)):
    out = jnp.where(row == i, jnp.broadcast_to(part.astype(F32), (BF16_ROWS, HD)), out)
  return out.astype(BF16)


def _over_dims(x):
  return x[:, None, :]


def _page_scores(qrep, k_refs):
  rows = [[] for _ in k_refs]
  for h in range(N_HEADS):
    qh = qrep[h]
    for i, k_ref in enumerate(k_refs):
      rows[i].append(jnp.sum(qh * k_ref[h], axis=0, keepdims=True))
  return [jnp.concatenate(r, axis=0) for r in rows]


def _accumulate_values(acc_ref, weights, v_refs, scale=None):
  for h in range(N_HEADS):
    upd = acc_ref[h]
    if scale is not None:
      upd = upd * scale[h:h + 1, :]
    for w, v_ref in zip(weights, v_refs):
      upd = upd + w[h:h + 1, :] * v_ref[h]
    acc_ref[h] = upd


def _sb_decode_init(q_row, qrep, r_ref, acc_ref):
  qrep[...] = _lane_replicated(_row0(q_row))
  r_ref[...] = jnp.zeros(r_ref.shape, F32)
  acc_ref[...] = jnp.zeros(acc_ref.shape, F32)


def _sb_decode_pages(qrep, r_ref, acc_ref, uw_ref, k_pages, v_pages):
  weights = []
  r = r_ref[...]
  for z in _page_scores(qrep, k_pages):
    hi, lo = _split2(_neg_softplus(z))
    ct = _dot(jnp.concatenate([hi, lo], axis=1), uw_ref[...])
    weights.append(jnp.exp(z + ct[:, 0:BLK] + r))
    r = r + ct[:, BLK:2 * BLK]
  r_ref[...] = r
  _accumulate_values(acc_ref, weights, v_pages)


def _physical_pages(pool):
  return jnp.transpose(pool, (0, 2, 3, 1))


def _fox_decode_init(q_row, kn_row, vn_row, lfn, qrep, d_ref, m_ref, l_ref, acc_ref):
  qr = _lane_replicated(_row0(q_row))
  qrep[...] = qr
  m_ref[...] = jnp.sum(qr * _lane_replicated(_rows_split3(kn_row)), axis=1)
  l_ref[...] = jnp.ones(l_ref.shape, F32)
  lane = lax.broadcasted_iota(jnp.int32, acc_ref.shape, 2)
  acc_ref[...] = jnp.where(lane == 0, _lane_replicated(_rows_split3(vn_row)), 0.0)
  d_ref[...] = jnp.broadcast_to(lfn, d_ref.shape)


def _fox_decode_pages(qrep, d_ref, m_ref, l_ref, acc_ref, us_ref, k_pages, v_pages, f_pages):
  scores = []
  d = d_ref[...]
  for z, f_ref in zip(_page_scores(qrep, k_pages), f_pages):
    dt = _dot(jnp.concatenate(_split3(f_ref[...]), axis=1), us_ref[...])
    scores.append(z + dt[:, 0:BLK] + d)
    d = d + dt[:, BLK:2 * BLK]
  d_ref[...] = d
  m_old = m_ref[...]
  s_max = functools.reduce(jnp.maximum, scores)
  m_new = jnp.maximum(m_old, jnp.max(s_max, axis=1, keepdims=True))
  probs = [jnp.exp(s - m_new) for s in scores]
  alpha = jnp.exp(m_old - m_new)
  m_ref[...] = m_new
  l_ref[...] = alpha * l_ref[...] + jnp.sum(functools.reduce(jnp.add, probs), axis=1, keepdims=True)
  _accumulate_values(acc_ref, probs, v_pages, scale=alpha)


def _fox_decode_result(l_ref, acc_ref):
  return jnp.sum(acc_ref[...] * _over_dims(1.0 / l_ref[...]), axis=2, keepdims=True)


def _forget_suffix_weights():
  j = jnp.arange(BLK)
  us = jnp.concatenate([(j[:, None] > j[None, :]).astype(BF16), jnp.ones((BLK, BLK), BF16)], axis=1)
  return jnp.concatenate([us] * 3, axis=0)


def kernel(x_prompt, x_sample, cache_sb_k, cache_sb_v, cache_fox_k, cache_fox_v, cache_fox_logf,
           state_conv, page_table, meta_tokens, norm_mix, norm_ffn, w_qkv_sb, w_o_sb, w_in_fox,
           b_forget, q_norm_g, k_norm_g, w_o_fox, w_up, conv_w, conv_b, w_down, norm_final):
  b, seq, d = x_prompt.shape
  n = x_sample.shape[0]
  assert x_sample.shape[1] == 1 and d == HD
  t_len = N_META + seq
  depth = norm_mix.shape[0]
  assert depth == 2 and w_qkv_sb.shape[0] == 1 and w_in_fox.shape[0] == 1

  meta = jnp.broadcast_to(meta_tokens.astype(x_prompt.dtype)[None], (b, N_META, d))
  hp = jnp.concatenate([meta, x_prompt], axis=1)
  hs = x_sample.reshape(n, d)

  heads = lambda a, lead: a.reshape(lead + (N_HEADS, HEAD_DIM))[None]
  ffn = [_ffn_weights(w_up[i], conv_w[i], conv_b[i], w_down[i]) for i in range(depth)]
  conv_p, conv_s = [], []

  w = w_qkv_sb[0].astype(BF16)
  wo = w_o_sb[0].astype(BF16)
  q, k, v = _proj_sb(hp.reshape(b * t_len, d), norm_mix[0], w)
  q_s, k_s, v_s = _proj_sb(hs, norm_mix[0], w)
  sbk_s, sbv_s = heads(k_s, (n, 1)), heads(v_s, (n, 1))
  o, kt, vt, o_s = _sb_attn(q.reshape(b, t_len, HD), k.reshape(b, t_len, HD),
                            v.reshape(b, t_len, HD), q_s, cache_sb_k[0], cache_sb_v[0], page_table)
  sbk_p, sbv_p = _cache_order(kt), _cache_order(vt)
  hp, cp = _post_prompt(hp, o, wo, norm_ffn[0], ffn[0])
  conv_p.append(cp)
  hs, cs = _post_sample(hs, o_s, wo, norm_ffn[0], ffn[0], state_conv[0])
  conv_s.append(cs)

  w = w_in_fox[0, :, :3 * HD].astype(BF16)
  wf = jnp.pad(w_in_fox[0, :, 3 * HD:], ((0, 0), (0, LANES - N_HEADS))).astype(BF16)
  wo = w_o_fox[0].astype(BF16)
  fox_args = (norm_mix[1], w, wf, b_forget[0], q_norm_g[0], k_norm_g[0])
  q, k, v, lf = _proj_fox(hp.reshape(b * t_len, d), *fox_args)
  ff_p = lf.reshape(1, b, t_len, N_HEADS)
  q_s, k_s, v_s, lf_s = _proj_fox(hs, *fox_args)
  fk_s, fv_s = heads(k_s, (n, 1)), heads(v_s, (n, 1))
  ff_s = lf_s.reshape(1, n, 1, N_HEADS)
  o, kt, vt, o_s = _fox_attn(q.reshape(b, t_len, HD), k.reshape(b, t_len, HD),
                             v.reshape(b, t_len, HD), lf.reshape(b, t_len, N_HEADS),
                             q_s, k_s, v_s, lf_s,
                             cache_fox_k[0], cache_fox_v[0], cache_fox_logf[0], page_table)
  fk_p, fv_p = _cache_order(kt), _cache_order(vt)
  hp, cp, y_full = _post_prompt(hp, o, wo, norm_ffn[1], ffn[1], norm_final)
  conv_p.append(cp)
  hs, cs, y_s = _post_sample(hs, o_s, wo, norm_ffn[1], ffn[1], state_conv[1], norm_final)
  conv_s.append(cs)

  y_prompt = y_full[:, N_META:]
  y_sample = y_s.reshape(n, 1, d)
  return (y_prompt, y_sample, sbk_p, sbv_p, fk_p, fv_p, ff_p, jnp.stack(conv_p),
          sbk_s, sbv_s, fk_s, fv_s, ff_s, jnp.stack(conv_s))
```

```python
import functools

import jax
import jax.numpy as jnp
from jax import lax
from jax.experimental import pallas as pl
from jax.experimental.pallas import tpu as pltpu

F32 = jnp.float32
BF16 = jnp.bfloat16

N_HEADS = 16
HEAD_DIM = 64
HD = N_HEADS * HEAD_DIM
N_META = 16
CONV_WIDTH = 3
RMS_EPS = 1e-6
ATTN_SCALE = HEAD_DIM ** -0.5

LANES = 128
SUBLANES = 8
BF16_ROWS = 16
MXU_DIM = 256
BLK = LANES
QBLK = 2 * BLK
PAIR = 2 * HEAD_DIM
PAIRS_PER_STEP = 2
V7X_SCOPED_VMEM_BYTES = 60000 * 1024
NEG_BIG = -1e30

ROW_TILE_CAP = 688
FFN_CHUNK = MXU_DIM
DECODE_PAGES_PER_STEP = 8


def _row_tile(m, cap=ROW_TILE_CAP):
  best = None
  for t in range(BF16_ROWS, min(m, cap) + 1, BF16_ROWS):
    if m % t == 0:
      best = t
  assert best is not None, m
  return best


def _params(n_axes, vmem_bytes):
  assert vmem_bytes <= V7X_SCOPED_VMEM_BYTES
  return pltpu.CompilerParams(
      dimension_semantics=("arbitrary",) * n_axes, vmem_limit_bytes=int(vmem_bytes))


def _resident(shape):
  zeros = (0,) * len(shape)
  return pl.BlockSpec(shape, lambda *_: zeros, pipeline_mode=pl.Buffered(1))


def _rms(x, g):
  ms = jnp.mean(x * x, axis=-1, keepdims=True)
  return x * lax.rsqrt(ms + RMS_EPS) * g


def _neg_softplus(z):
  return -(jnp.maximum(z, 0.0) + jnp.log(1.0 + jnp.exp(-jnp.abs(z))))


def _split2(x):
  hi = x.astype(BF16)
  lo = (x - hi.astype(F32)).astype(BF16)
  return hi, lo


def _split3(x):
  hi = x.astype(BF16)
  r = x - hi.astype(F32)
  mid = r.astype(BF16)
  lo = (r - mid.astype(F32)).astype(BF16)
  return hi, mid, lo


def _dot(a, b):
  return jnp.dot(a, b, preferred_element_type=F32)


def _dot_nt(a, b):
  return lax.dot_general(a, b, (((1,), (1,)), ((), ())), preferred_element_type=F32)


def _dot_tn(a, b):
  return lax.dot_general(a, b, (((0,), (0,)), ((), ())), preferred_element_type=F32)


def _proj_sb_kernel(x_ref, g_ref, w_ref, q_ref, k_ref, v_ref):
  xn = _rms(x_ref[...], g_ref[...]).astype(BF16)
  q_ref[...] = (_dot(xn, w_ref[:, 0:HD]) * ATTN_SCALE).astype(BF16)
  k_ref[...] = _dot(xn, w_ref[:, HD:2 * HD])
  v_ref[...] = _dot(xn, w_ref[:, 2 * HD:3 * HD])


def _proj_sb(x2d, g, w):
  m, d = x2d.shape
  tm = _row_tile(m)
  row = lambda i: (i, 0)
  vmem = 2 * tm * (d * 4 + HD * 2 + 2 * HD * 4) + d * 3 * HD * 2 + 6 * tm * HD * 4
  return pl.pallas_call(
      _proj_sb_kernel,
      grid=(m // tm,),
      in_specs=[pl.BlockSpec((tm, d), row), _resident((1, d)), _resident((d, 3 * HD))],
      out_specs=[pl.BlockSpec((tm, HD), row)] * 3,
      out_shape=[jax.ShapeDtypeStruct((m, HD), BF16),
                 jax.ShapeDtypeStruct((m, HD), F32),
                 jax.ShapeDtypeStruct((m, HD), F32)],
      compiler_params=_params(1, vmem + (4 << 20)),
      name="proj_sb",
  )(x2d, g.reshape(1, d), w)


def _proj_fox_kernel(x_ref, g_ref, w_ref, wf_ref, bf_ref, gq_ref, gk_ref, s_ref,
                     q_ref, k_ref, v_ref, lf_ref):
  xn = _rms(x_ref[...], g_ref[...]).astype(BF16)

  def head_norm(y, gain):
    y2 = (y * y).astype(BF16)
    ss = jnp.concatenate(
        [_dot(y2[:, c * MXU_DIM:(c + 1) * MXU_DIM], s_ref[...]) for c in range(HD // MXU_DIM)],
        axis=1)
    return y * lax.rsqrt(ss * (1.0 / HEAD_DIM) + RMS_EPS) * gain

  q = head_norm(_dot(xn, w_ref[:, 0:HD]), gq_ref[...])
  q_ref[...] = (q * ATTN_SCALE).astype(BF16)
  k_ref[...] = head_norm(_dot(xn, w_ref[:, HD:2 * HD]), gk_ref[...])
  v_ref[...] = _dot(xn, w_ref[:, 2 * HD:3 * HD])
  f = _dot(xn, wf_ref[...])[:, 0:N_HEADS] + bf_ref[...]
  lf_ref[...] = jnp.minimum(f, 0.0) - jnp.log(1.0 + jnp.exp(-jnp.abs(f)))


def _proj_fox(x2d, g, w, wf, bf, gq, gk):
  m, d = x2d.shape
  tm = _row_tile(m)
  row = lambda i: (i, 0)
  r = jnp.arange(MXU_DIM) // HEAD_DIM
  seg = (r[:, None] == r[None, :]).astype(BF16)
  vmem = 2 * tm * (d * 4 + HD * 2 + 2 * HD * 4 + LANES * 4) + d * (3 * HD + LANES) * 2 + 8 * tm * HD * 4
  return pl.pallas_call(
      _proj_fox_kernel,
      grid=(m // tm,),
      in_specs=[pl.BlockSpec((tm, d), row), _resident((1, d)), _resident((d, 3 * HD)),
                _resident((d, LANES)), _resident((1, N_HEADS)), _resident((1, HD)),
                _resident((1, HD)), _resident((MXU_DIM, MXU_DIM))],
      out_specs=[pl.BlockSpec((tm, HD), row)] * 3 + [pl.BlockSpec((tm, N_HEADS), row)],
      out_shape=[jax.ShapeDtypeStruct((m, HD), BF16),
                 jax.ShapeDtypeStruct((m, HD), F32),
                 jax.ShapeDtypeStruct((m, HD), F32),
                 jax.ShapeDtypeStruct((m, N_HEADS), F32)],
      compiler_params=_params(1, min(vmem + (4 << 20), V7X_SCOPED_VMEM_BYTES)),
      name="proj_fox",
  )(x2d, g.reshape(1, d), w, wf, bf.reshape(1, N_HEADS),
    jnp.tile(gq, N_HEADS).reshape(1, HD), jnp.tile(gk, N_HEADS).reshape(1, HD), seg)


def _attn_geometry(t_len):
  nq, tail = divmod(t_len, QBLK)
  assert tail % BF16_ROWS == 0 and tail <= BLK
  return nq, tail, 2 * nq + (1 if tail else 0)


def _pair_lanes(n_par):
  return [slice(p * PAIR, (p + 1) * PAIR) for p in range(n_par)]


def _stage_kv(k_ref, v_ref, kt, vab, kout_ref, vout_ref, t_len, lanes):
  _, _, nb = _attn_geometry(t_len)
  head0 = lax.broadcasted_iota(jnp.int32, (BLK, PAIR), 1) < HEAD_DIM
  top = lax.broadcasted_iota(jnp.int32, (PAIR, BLK), 0) < HEAD_DIM
  for j in range(nb):
    rows = min(BLK, t_len - j * BLK)
    kb = k_ref[0, j * BLK:j * BLK + rows, lanes]
    vb = v_ref[0, j * BLK:j * BLK + rows, lanes]
    if rows < BLK:
      zeros = jnp.zeros((BLK - rows, PAIR), F32)
      kb = jnp.concatenate([kb, zeros], axis=0)
      vb = jnp.concatenate([vb, zeros], axis=0)
    kbt = kb.T
    kt[j] = jnp.concatenate([jnp.where(top, kbt, 0.0), jnp.where(top, 0.0, kbt)], axis=1).astype(BF16)
    vab[j * 2 * BLK:j * 2 * BLK + BLK, :] = jnp.where(head0, vb, 0.0).astype(BF16)
    vab[j * 2 * BLK + BLK:(j + 1) * 2 * BLK, :] = jnp.where(head0, 0.0, vb).astype(BF16)
    kout_ref[0, lanes, j * BLK:j * BLK + rows] = kbt[:, 0:rows]
    vout_ref[0, lanes, j * BLK:j * BLK + rows] = vb.T[:, 0:rows]


def _for_each_qblock(t_len, fn, hook=None):
  nq, tail, _ = _attn_geometry(t_len)

  def body(i, c):
    if hook is not None:
      hook(i)
    fn(pl.multiple_of(i * QBLK, QBLK), QBLK, 2 * i, 2, i)
    return c

  lax.fori_loop(0, nq, body, 0)
  if tail:
    fn(nq * QBLK, tail, 2 * nq, 1, int(nq))


def _visit_earlier_keys(kv_step, jband, n_pairs):
  if isinstance(n_pairs, int):
    if n_pairs:
      kv_step(0, 2 * n_pairs, False)
    return

  def body(t, c):
    kv_step(jband - 2 - 2 * t, 2, False)
    return c

  lax.fori_loop(0, n_pairs, body, 0)


def _band_masks(tq, nblk, inclusive):
  row = lax.broadcasted_iota(jnp.int32, (tq, BLK), 0)
  col = lax.broadcasted_iota(jnp.int32, (tq, BLK), 1)
  blocks = [(u * BLK + col <= row) if inclusive else (u * BLK + col < row) for u in range(nblk)]
  return blocks, jnp.concatenate([m for m in blocks for _ in range(2)], axis=1)


def _pair_scores(qb, kt, j0, nblk):
  return jnp.concatenate([_dot(qb, kt[j0 + u]) for u in range(nblk)], axis=1)


def _sb_kv_step(qb, kt, vab, r_ref, acc_ref, uw_ref, tq, j0, nblk, band):
  z = _pair_scores(qb, kt, j0, nblk)
  l1m = _neg_softplus(z)
  if band:
    valid, valid_all = _band_masks(tq, nblk, inclusive=False)
    l1m = jnp.where(valid_all, l1m, 0.0)
  hi, lo = _split2(l1m)
  r_old = r_ref[0:tq, :]
  a_parts = [None] * (2 * nblk)
  for h in range(2):
    r_h = r_old[:, h * BLK:(h + 1) * BLK]
    for u in reversed(range(nblk)):
      sl = slice((2 * u + h) * BLK, (2 * u + h + 1) * BLK)
      ct = _dot(jnp.concatenate([hi[:, sl], lo[:, sl]], axis=1), uw_ref[...])
      a = jnp.exp(z[:, sl] + ct[:, 0:BLK] + r_h)
      if band:
        a = jnp.where(valid[u], a, 0.0)
      a_parts[2 * u + h] = a
      r_h = r_h + ct[:, BLK:2 * BLK]
    r_ref[0:tq, h * BLK:(h + 1) * BLK] = r_h
  a_all = jnp.concatenate(a_parts, axis=1).astype(BF16)
  d0 = pl.multiple_of(j0 * 2 * BLK, 2 * BLK)
  acc_ref[0:tq, :] += _dot(a_all, vab[pl.ds(d0, nblk * 2 * BLK), :])


def _page_copies(pt_ref, seq, chunk, pools, bufs, sems, slot, n_pages, pages):
  copies = []
  for pg in range(pages):
    page = pt_ref[seq, n_pages - 1 - (chunk * pages + pg)]
    for t, (pool, buf) in enumerate(zip(pools, bufs)):
      copies.append(pltpu.make_async_copy(pool.at[page], buf.at[slot, pg], sems.at[slot, t]))
  return copies


def _sb_attn_kernel(pt_ref, q_ref, k_ref, v_ref, uw_ref, qd_ref, kpool, vpool,
                    o_ref, kout_ref, vout_ref, od_ref,
                    kt, vab, r_ref, acc_ref, qrep, rd_ref, accd_ref, kbuf, vbuf, sems,
                    *, t_len, n_pages, pages):
  seq = pl.program_id(0) * pl.num_programs(1) + pl.program_id(1)
  n_chunks = n_pages // pages
  copies = lambda chunk, slot: _page_copies(
      pt_ref, seq, chunk, (kpool, vpool), (kbuf, vbuf), sems, slot, n_pages, pages)
  assert n_chunks >= 2
  for slot in range(2):
    for c in copies(slot, slot):
      c.start()
  _sb_decode_init(qd_ref[0], qrep, rd_ref, accd_ref)

  def decode_chunk(i):
    slot = lax.rem(i, 2)
    for c in copies(i, slot):
      c.wait()
    _sb_decode_pages(qrep, rd_ref, accd_ref, uw_ref,
                     [kbuf.at[slot, pg] for pg in range(pages)],
                     [vbuf.at[slot, pg] for pg in range(pages)])

    @pl.when(i + 2 < n_chunks)
    def _():
      for c in copies(i + 2, slot):
        c.start()

  n_par = kt.shape[0]
  lanes = _pair_lanes(n_par)
  for p in range(n_par):
    _stage_kv(k_ref, v_ref, kt.at[p], vab.at[p], kout_ref, vout_ref, t_len, lanes[p])

  def qblock(q0, tq, jband, nband, n_pairs):
    qbs = [q_ref[0, pl.ds(q0, tq), lanes[p]] for p in range(n_par)]
    for p in range(n_par):
      acc_ref[p, 0:tq, :] = jnp.zeros((tq, PAIR), F32)
      r_ref[p, 0:tq, :] = jnp.zeros((tq, 2 * BLK), F32)

    def kv_step(j0, nblk, band):
      for p in range(n_par):
        _sb_kv_step(qbs[p], kt.at[p], vab.at[p], r_ref.at[p], acc_ref.at[p], uw_ref,
                    tq, j0, nblk, band)

    kv_step(jband, nband, True)
    _visit_earlier_keys(kv_step, jband, n_pairs)
    for p in range(n_par):
      o_ref[0, pl.ds(q0, tq), lanes[p]] = acc_ref[p, 0:tq, :].astype(BF16)

  _for_each_qblock(t_len, qblock, hook=decode_chunk)
  od_ref[0] = jnp.sum(accd_ref[...], axis=2, keepdims=True)


def _suffix_sum_weights():
  j = jnp.arange(BLK)
  u = (j[:, None] >= j[None, :]).astype(BF16)
  half = jnp.concatenate([u, jnp.ones((BLK, BLK), BF16)], axis=1)
  return jnp.concatenate([half, half], axis=0)


def _attn_specs(b, t_len):
  width = PAIRS_PER_STEP * PAIR
  blk = pl.BlockSpec((1, t_len, width), lambda i, p, *_: (i, 0, p))
  blk_t = pl.BlockSpec((1, width, t_len), lambda i, p, *_: (i, p, 0))
  out_specs = [blk, blk_t, blk_t]
  out_shape = [jax.ShapeDtypeStruct((b, t_len, HD), BF16),
               jax.ShapeDtypeStruct((b, HD, t_len), F32),
               jax.ShapeDtypeStruct((b, HD, t_len), F32)]
  return blk, out_specs, out_shape


def _cache_order(xt):
  b, _, t_len = xt.shape
  return jnp.transpose(xt.reshape(b, N_HEADS, HEAD_DIM, t_len), (0, 3, 1, 2))[None]


def _sb_attn(q, k, v, q_dec, k_pool, v_pool, page_table):
  b, t_len, _ = q.shape
  nq, _, nb = _attn_geometry(t_len)
  n_par = PAIRS_PER_STEP
  n_groups = HD // (n_par * PAIR)
  n = q_dec.shape[0]
  page_size = k_pool.shape[1]
  n_pages = page_table.shape[1]
  pages = DECODE_PAGES_PER_STEP
  assert page_size == BLK and n == b * n_groups and n_pages == nq * pages
  page_block = (N_HEADS, HEAD_DIM, page_size)
  page_bytes = N_HEADS * HEAD_DIM * page_size * 4
  blk, out_specs, out_shape = _attn_specs(b, t_len)
  seq = lambda i, p, pt: i * n_groups + p
  grid_spec = pltpu.PrefetchScalarGridSpec(
      num_scalar_prefetch=1,
      grid=(b, n_groups),
      in_specs=[blk, blk, blk, _resident((2 * BLK, 2 * BLK)),
                pl.BlockSpec((1, 1, HD), lambda i, p, pt: (seq(i, p, pt), 0, 0)),
                pl.BlockSpec(memory_space=pl.ANY), pl.BlockSpec(memory_space=pl.ANY)],
      out_specs=out_specs + [pl.BlockSpec((1, N_HEADS, HEAD_DIM, 1),
                                          lambda i, p, pt: (seq(i, p, pt), 0, 0, 0))],
      scratch_shapes=[pltpu.VMEM((n_par, nb, PAIR, 2 * BLK), BF16),
                      pltpu.VMEM((n_par, nb * 2 * BLK, PAIR), BF16),
                      pltpu.VMEM((n_par, QBLK, 2 * BLK), F32),
                      pltpu.VMEM((n_par, QBLK, PAIR), F32),
                      pltpu.VMEM(page_block, F32),
                      pltpu.VMEM((N_HEADS, BLK), F32),
                      pltpu.VMEM(page_block, F32),
                      pltpu.VMEM((2, pages) + page_block, F32),
                      pltpu.VMEM((2, pages) + page_block, F32),
                      pltpu.SemaphoreType.DMA((2, 2))],
  )
  o, kt, vt, o_dec = pl.pallas_call(
      functools.partial(_sb_attn_kernel, t_len=t_len, n_pages=n_pages, pages=pages),
      grid_spec=grid_spec,
      out_shape=out_shape + [jax.ShapeDtypeStruct((n, N_HEADS, HEAD_DIM, 1), F32)],
      compiler_params=_params(2, (36 << 20) + 2 * 2 * pages * page_bytes),
      name="sb_attn",
  )(page_table, q, k, v, _suffix_sum_weights(), q_dec.reshape(n, 1, HD),
    _physical_pages(k_pool), _physical_pages(v_pool))
  return o, kt, vt, o_dec.reshape(n, HD).astype(BF16)


def _stage_forget_bias(lf_ref, ltri_ref, cq, ckt, t_len):
  nb = _attn_geometry(t_len)[2]
  carry = jnp.zeros((1, 2 * BLK), F32)
  for j in range(nb):
    rows = min(BLK, t_len - j * BLK)
    lf = lf_ref[j * BLK:j * BLK + rows, :]
    x = jnp.concatenate([jnp.broadcast_to(lf[:, h:h + 1], (rows, BLK)) for h in range(2)], axis=1)
    if rows < BLK:
      x = jnp.concatenate([x, jnp.zeros((BLK - rows, 2 * BLK), F32)], axis=0)
    c = carry
    for part in _split3(x):
      c = c + _dot(ltri_ref[...], part)
    cq[j * BLK:(j + 1) * BLK, :] = c
    carry = c[BLK - 1:BLK, :]
  for j in range(nb):
    cb = cq[j * BLK:(j + 1) * BLK, :]
    ckt[j] = jnp.concatenate([cb[:, 0:BLK].T[0:SUBLANES], cb[:, BLK:2 * BLK].T[0:SUBLANES]], axis=1)


def _fox_kv_step(qb, cqb, kt, vab, ckt, m_ref, l_ref, acc_ref, tq, j0, nblk, band):
  head0 = lax.broadcasted_iota(jnp.int32, (tq, PAIR), 1) < HEAD_DIM
  d0 = pl.multiple_of(j0 * 2 * BLK, 2 * BLK)
  ck = jnp.concatenate([ckt[j0 + u][0:1, :] for u in range(nblk)], axis=1)
  s = _pair_scores(qb, kt, j0, nblk) + jnp.concatenate([cqb] * nblk, axis=1) - ck
  if band:
    s = jnp.where(_band_masks(tq, nblk, inclusive=True)[1], s, NEG_BIG)
  m_old = m_ref[0:tq, :]
  l_old = l_ref[0:tq, :]
  p_parts = [None] * (2 * nblk)
  alphas = []
  for h in range(2):
    sl = slice(h * BLK, (h + 1) * BLK)
    cols = [slice(u * 2 * BLK + h * BLK, u * 2 * BLK + (h + 1) * BLK) for u in range(nblk)]
    s_h = jnp.concatenate([s[:, c] for c in cols], axis=1)
    m_new = jnp.maximum(m_old[:, sl], jnp.max(s_h, axis=1, keepdims=True))
    p_acc = None
    for u in range(nblk):
      p = jnp.exp(s[:, cols[u]] - m_new)
      p_parts[2 * u + h] = p
      p_acc = p if p_acc is None else p_acc + p
    alpha = jnp.exp(m_old[:, sl] - m_new)
    m_ref[0:tq, sl] = m_new
    l_ref[0:tq, sl] = alpha * l_old[:, sl] + jnp.sum(p_acc, axis=1, keepdims=True)
    alphas.append(alpha)
  p_all = jnp.concatenate(p_parts, axis=1).astype(BF16)
  scale = jnp.where(head0, alphas[0], alphas[1])
  acc_ref[0:tq, :] = acc_ref[0:tq, :] * scale + _dot(p_all, vab[pl.ds(d0, nblk * 2 * BLK), :])


def _fox_attn_kernel(pt_ref, q_ref, k_ref, v_ref, lf_ref, ltri_ref,
                     qd_ref, knd_ref, vnd_ref, lfnd_ref, us_ref, kpool, vpool, fpool,
                     o_ref, kout_ref, vout_ref, od_ref,
                     kt, vab, cq, ckt, m_ref, l_ref, acc_ref,
                     qrep, dd_ref, md_ref, ld_ref, accd_ref, kbuf, vbuf, fbuf, sems,
                     *, t_len, n_pages, pages):
  seq = pl.program_id(0) * pl.num_programs(1) + pl.program_id(1)
  n_chunks = n_pages // pages
  copies = lambda chunk, slot: _page_copies(
      pt_ref, seq, chunk, (kpool, vpool, fpool), (kbuf, vbuf, fbuf), sems, slot, n_pages, pages)
  assert n_chunks >= 2
  for slot in range(2):
    for c in copies(slot, slot):
      c.start()
  _fox_decode_init(qd_ref[0], knd_ref[0], vnd_ref[0], lfnd_ref[0],
                   qrep, dd_ref, md_ref, ld_ref, accd_ref)

  def decode_chunk(i):
    slot = lax.rem(i, 2)
    for c in copies(i, slot):
      c.wait()
    _fox_decode_pages(qrep, dd_ref, md_ref, ld_ref, accd_ref, us_ref,
                      [kbuf.at[slot, pg] for pg in range(pages)],
                      [vbuf.at[slot, pg] for pg in range(pages)],
                      [fbuf.at[slot, pg] for pg in range(pages)])

    @pl.when(i + 2 < n_chunks)
    def _():
      for c in copies(i + 2, slot):
        c.start()

  n_par = kt.shape[0]
  lanes = _pair_lanes(n_par)
  for p in range(n_par):
    _stage_kv(k_ref, v_ref, kt.at[p], vab.at[p], kout_ref, vout_ref, t_len, lanes[p])
    _stage_forget_bias(lf_ref.at[0, p], ltri_ref, cq.at[p], ckt.at[p], t_len)

  def qblock(q0, tq, jband, nband, n_pairs):
    head0 = lax.broadcasted_iota(jnp.int32, (tq, PAIR), 1) < HEAD_DIM
    qbs = [q_ref[0, pl.ds(q0, tq), lanes[p]] for p in range(n_par)]
    cqbs = [cq[p, pl.ds(q0, tq), :] for p in range(n_par)]
    for p in range(n_par):
      acc_ref[p, 0:tq, :] = jnp.zeros((tq, PAIR), F32)
      m_ref[p, 0:tq, :] = jnp.full((tq, 2 * BLK), NEG_BIG, F32)
      l_ref[p, 0:tq, :] = jnp.zeros((tq, 2 * BLK), F32)

    def kv_step(j0, nblk, band):
      for p in range(n_par):
        _fox_kv_step(qbs[p], cqbs[p], kt.at[p], vab.at[p], ckt.at[p], m_ref.at[p], l_ref.at[p],
                     acc_ref.at[p], tq, j0, nblk, band)

    kv_step(jband, nband, True)
    _visit_earlier_keys(kv_step, jband, n_pairs)
    for p in range(n_par):
      l_fin = l_ref[p, 0:tq, :]
      denom = jnp.where(head0, l_fin[:, 0:BLK], l_fin[:, BLK:2 * BLK])
      o_ref[0, pl.ds(q0, tq), lanes[p]] = (acc_ref[p, 0:tq, :] / denom).astype(BF16)

  _for_each_qblock(t_len, qblock, hook=decode_chunk)
  od_ref[0] = _fox_decode_result(ld_ref, accd_ref)


def _fox_attn(q, k, v, logf, q_dec, k_new, v_new, lf_new, k_pool, v_pool, f_pool, page_table):
  b, t_len, _ = q.shape
  nq, _, nb = _attn_geometry(t_len)
  n_pairs = HD // PAIR
  n_par = PAIRS_PER_STEP
  n_groups = n_pairs // n_par
  n = q_dec.shape[0]
  page_size = k_pool.shape[1]
  n_pages = page_table.shape[1]
  pages = DECODE_PAGES_PER_STEP
  assert page_size == BLK and n == b * n_groups and n_pages == nq * pages
  page_block = (N_HEADS, HEAD_DIM, page_size)
  page_bytes = N_HEADS * HEAD_DIM * page_size * 4
  lf = logf.reshape(b, t_len, n_pairs, 2).transpose(0, 2, 1, 3)
  j = jnp.arange(BLK)
  ltri = (j[:, None] >= j[None, :]).astype(BF16)
  blk, out_specs, out_shape = _attn_specs(b, t_len)
  once = pl.BlockSpec(blk.block_shape, blk.index_map, pipeline_mode=pl.Buffered(1))
  seq = lambda i, p: i * n_groups + p
  row = pl.BlockSpec((1, 1, HD), lambda i, p, pt: (seq(i, p), 0, 0))
  hbm = pl.BlockSpec(memory_space=pl.ANY)
  grid_spec = pltpu.PrefetchScalarGridSpec(
      num_scalar_prefetch=1,
      grid=(b, n_groups),
      in_specs=[blk, once, once,
                pl.BlockSpec((1, n_par, t_len, 2), lambda i, p, pt: (i, p, 0, 0),
                             pipeline_mode=pl.Buffered(1)),
                _resident((BLK, BLK)),
                row, row, row,
                pl.BlockSpec((1, N_HEADS, 1), lambda i, p, pt: (seq(i, p), 0, 0)),
                _resident((3 * BLK, 2 * BLK)), hbm, hbm, hbm],
      out_specs=out_specs + [pl.BlockSpec((1, N_HEADS, HEAD_DIM, 1),
                                          lambda i, p, pt: (seq(i, p), 0, 0, 0))],
      scratch_shapes=[pltpu.VMEM((n_par, nb, PAIR, 2 * BLK), BF16),
                      pltpu.VMEM((n_par, nb * 2 * BLK, PAIR), BF16),
                      pltpu.VMEM((n_par, nb * BLK, 2 * BLK), F32),
                      pltpu.VMEM((n_par, nb, SUBLANES, 2 * BLK), F32),
                      pltpu.VMEM((n_par, QBLK, 2 * BLK), F32),
                      pltpu.VMEM((n_par, QBLK, 2 * BLK), F32),
                      pltpu.VMEM((n_par, QBLK, PAIR), F32),
                      pltpu.VMEM(page_block, F32),
                      pltpu.VMEM((N_HEADS, BLK), F32),
                      pltpu.VMEM((N_HEADS, BLK), F32),
                      pltpu.VMEM((N_HEADS, BLK), F32),
                      pltpu.VMEM(page_block, F32),
                      pltpu.VMEM((2, pages) + page_block, F32),
                      pltpu.VMEM((2, pages) + page_block, F32),
                      pltpu.VMEM((2, pages, N_HEADS, page_size), F32),
                      pltpu.SemaphoreType.DMA((2, 3))],
  )
  o, kt, vt, o_dec = pl.pallas_call(
      functools.partial(_fox_attn_kernel, t_len=t_len, n_pages=n_pages, pages=pages),
      grid_spec=grid_spec,
      out_shape=out_shape + [jax.ShapeDtypeStruct((n, N_HEADS, HEAD_DIM, 1), F32)],
      compiler_params=_params(2, min((42 << 20) + 2 * 2 * pages * page_bytes, V7X_SCOPED_VMEM_BYTES)),
      name="fox_attn",
  )(page_table, q, k, v, lf, ltri, q_dec.reshape(n, 1, HD), k_new.reshape(n, 1, HD),
    v_new.reshape(n, 1, HD), lf_new.reshape(n, N_HEADS, 1), _forget_suffix_weights(),
    _physical_pages(k_pool), _physical_pages(v_pool), jnp.transpose(f_pool, (0, 2, 1)))
  return o, kt, vt, o_dec.reshape(n, HD).astype(BF16)


def _silu(x):
  return x * (1.0 / (1.0 + jnp.exp(-x)))


def _post_prompt_kernel(h_ref, o_ref, wo_ref, g_ref, wg_ref, wv_ref, cw_ref, cb_ref, wd_ref,
                        *rest, final):
  if final:
    gfin_ref, hout_ref, conv_ref, y_ref, gbuf0, gbuf1, vbuf0, vbuf1, carry = rest
  else:
    hout_ref, conv_ref, gbuf0, gbuf1, vbuf0, vbuf1, carry = rest
  t = pl.program_id(1)
  tm = h_ref.shape[1]
  n_chunks = wg_ref.shape[0]
  pad = SUBLANES

  @pl.when(t == 0)
  def _():
    carry[...] = jnp.zeros(carry.shape, F32)

  h1 = h_ref[0] + _dot(o_ref[0], wo_ref[...])
  hn = _rms(h1, g_ref[...]).astype(BF16)
  hout_ref[0] = h1

  def up_proj(c, gbuf, vbuf):
    gbuf[0:pad, :] = carry[c]
    gbuf[pad:pad + tm, :] = _dot(hn, wg_ref[c])
    vbuf[...] = _dot(hn, wv_ref[c])

  def conv_down(c, gbuf, vbuf):
    w = cw_ref[c]
    gc = (cb_ref[c] + w[0:1] * gbuf[pad - 2:pad - 2 + tm, :]
          + w[1:2] * gbuf[pad - 1:pad - 1 + tm, :] + w[2:3] * gbuf[pad:pad + tm, :])
    act = (_silu(gc) * vbuf[...]).astype(BF16)
    hout_ref[0] += _dot(act, wd_ref[c])
    last = gbuf[tm:tm + pad, :]
    carry[c] = last
    conv_ref[0, c] = last[pad - (CONV_WIDTH - 1):pad, :]

  up_proj(0, gbuf0, vbuf0)

  def two_chunks(k, _):
    conv_down(2 * k, gbuf0, vbuf0)
    up_proj(2 * k + 1, gbuf1, vbuf1)
    conv_down(2 * k + 1, gbuf1, vbuf1)
    up_proj(2 * k + 2, gbuf0, vbuf0)
    return _

  lax.fori_loop(0, (n_chunks - 1) // 2, two_chunks, 0)
  if n_chunks % 2 == 0:
    conv_down(n_chunks - 2, gbuf0, vbuf0)
    up_proj(n_chunks - 1, gbuf1, vbuf1)
    conv_down(n_chunks - 1, gbuf1, vbuf1)
  else:
    conv_down(n_chunks - 1, gbuf0, vbuf0)
  if final:
    y_ref[0] = _rms(hout_ref[0], gfin_ref[...])


def _ffn_weights(w_up, conv_w, conv_b, w_down):
  d, two_ff = w_up.shape
  d_ff = two_ff // 2
  assert d_ff % FFN_CHUNK == 0
  nc = d_ff // FFN_CHUNK
  wg = w_up[:, :d_ff].astype(BF16).reshape(d, nc, FFN_CHUNK).transpose(1, 0, 2)
  wv = w_up[:, d_ff:].astype(BF16).reshape(d, nc, FFN_CHUNK).transpose(1, 0, 2)
  wd = w_down.astype(BF16).reshape(nc, FFN_CHUNK, d)
  cw = conv_w.reshape(CONV_WIDTH, nc, FFN_CHUNK).transpose(1, 0, 2)
  cb = conv_b.reshape(nc, 1, FFN_CHUNK)
  return wg, wv, cw, cb, wd


def _post_prompt(h, o, wo, g, ffn, g_final=None):
  b, t_len, d = h.shape
  wg, wv, cw, cb, wd = ffn
  nc = wg.shape[0]
  final = g_final is not None
  tm = _row_tile(t_len)
  tile = lambda i, t: (i, t, 0)
  act_spec = pl.BlockSpec((1, tm, d), tile)
  act_shape = jax.ShapeDtypeStruct((b, t_len, d), F32)
  w_bytes = 2 * (wo.size + wg.size + wv.size + wd.size)
  n_act = 3 if final else 2
  vmem = w_bytes + 2 * tm * d * (4 * n_act + 2) + 4 * tm * d * 4
  outs = pl.pallas_call(
      functools.partial(_post_prompt_kernel, final=final),
      grid=(b, t_len // tm),
      in_specs=[act_spec, pl.BlockSpec((1, tm, HD), tile), _resident(wo.shape), _resident((1, d)),
                _resident(wg.shape), _resident(wv.shape), _resident(cw.shape), _resident(cb.shape),
                _resident(wd.shape)] + ([_resident((1, d))] if final else []),
      out_specs=[act_spec,
                 pl.BlockSpec((1, nc, CONV_WIDTH - 1, FFN_CHUNK), lambda i, t: (i, 0, 0, 0))]
      + ([act_spec] if final else []),
      out_shape=[act_shape, jax.ShapeDtypeStruct((b, nc, CONV_WIDTH - 1, FFN_CHUNK), F32)]
      + ([act_shape] if final else []),
      scratch_shapes=[pltpu.VMEM((tm + SUBLANES, FFN_CHUNK), F32)] * 2
      + [pltpu.VMEM((tm, FFN_CHUNK), F32)] * 2
      + [pltpu.VMEM((nc, SUBLANES, FFN_CHUNK), F32)],
      compiler_params=_params(2, min(vmem, V7X_SCOPED_VMEM_BYTES)),
      name="post_prompt_final" if final else "post_prompt",
  )(h, o, wo, g.reshape(1, d), wg, wv, cw, cb, wd, *([g_final.reshape(1, d)] if final else []))
  conv = outs[1].transpose(0, 2, 1, 3).reshape(b, CONV_WIDTH - 1, nc * FFN_CHUNK)
  return (outs[0], conv) + tuple(outs[2:])


def _post_sample_kernel(h_ref, o_ref, wo_ref, g_ref, wg_ref, wv_ref, cw_ref, cb_ref, wd_ref,
                        prev_ref, *rest, final):
  if final:
    gfin_ref, hout_ref, conv_ref, y_ref = rest
  else:
    hout_ref, conv_ref = rest
  n_chunks = wg_ref.shape[0]
  h1 = h_ref[...] + _dot(o_ref[...], wo_ref[...])
  hn = _rms(h1, g_ref[...]).astype(BF16)
  acc = h1
  for c in range(n_chunks):
    g = _dot(hn, wg_ref[c])
    val = _dot(hn, wv_ref[c])
    w = cw_ref[c]
    prev0 = prev_ref[0, c]
    prev1 = prev_ref[1, c]
    gc = cb_ref[c] + w[0:1] * prev0 + w[1:2] * prev1 + w[2:3] * g
    acc = acc + _dot((_silu(gc) * val).astype(BF16), wd_ref[c])
    conv_ref[0, c] = prev1
    conv_ref[1, c] = g
  hout_ref[...] = acc
  if final:
    y_ref[...] = _rms(acc, gfin_ref[...])


def _post_sample(h, o, wo, g, ffn, state, g_final=None):
  n, d = h.shape
  wg, wv, cw, cb, wd = ffn
  nc = wg.shape[0]
  final = g_final is not None
  prev = state.reshape(n, CONV_WIDTH - 1, nc, FFN_CHUNK).transpose(1, 2, 0, 3)
  full = lambda shape: pl.BlockSpec(shape, lambda i: (0,) * len(shape))
  row_shape = jax.ShapeDtypeStruct((n, d), F32)
  w_bytes = 2 * (wo.size + wg.size + wv.size + wd.size)
  outs = pl.pallas_call(
      functools.partial(_post_sample_kernel, final=final),
      grid=(1,),
      in_specs=[full((n, d)), full((n, HD)), _resident(wo.shape), full((1, d)),
                _resident(wg.shape), _resident(wv.shape), full(cw.shape), full(cb.shape),
                _resident(wd.shape), full(prev.shape)] + ([full((1, d))] if final else []),
      out_specs=[full((n, d)), full(prev.shape)] + ([full((n, d))] if final else []),
      out_shape=[row_shape, jax.ShapeDtypeStruct(prev.shape, F32)] + ([row_shape] if final else []),
      compiler_params=_params(1, min(w_bytes + (12 << 20), V7X_SCOPED_VMEM_BYTES)),
      name="post_sample_final" if final else "post_sample",
  )(h, o, wo, g.reshape(1, d), wg, wv, cw, cb, wd, prev, *([g_final.reshape(1, d)] if final else []))
  conv = outs[1].transpose(2, 0, 1, 3).reshape(n, CONV_WIDTH - 1, nc * FFN_CHUNK)
  return (outs[0], conv) + tuple(outs[2:])


def _lane_replicated(rows):
  return _dot_tn(rows, jnp.ones((BF16_ROWS, LANES), BF16)).reshape(N_HEADS, HEAD_DIM, LANES)


def _row0(x):
  row = lax.broadcasted_iota(jnp.int32, (BF16_ROWS, HD), 0)
  return jnp.where(row == 0, jnp.broadcast_to(x.astype(F32), (BF16_ROWS, HD)), 0.0).astype(BF16)


def _rows_split3(x):
  row = lax.broadcasted_iota(jnp.int32, (BF16_ROWS, HD), 0)
  out = jnp.zeros((BF16_ROWS, HD), F32)
  for i, part in enumerate(_split3(x)):
    out = jnp.where(row == i, jnp.broadcast_to(part.astype(F32), (BF16_ROWS, HD)), out)
  return out.astype(BF16)


def _over_dims(x):
  return x[:, None, :]


def _page_scores(qrep, k_refs):
  rows = [[] for _ in k_refs]
  for h in range(N_HEADS):
    qh = qrep[h]
    for i, k_ref in enumerate(k_refs):
      rows[i].append(jnp.sum(qh * k_ref[h], axis=0, keepdims=True))
  return [jnp.concatenate(r, axis=0) for r in rows]


def _accumulate_values(acc_ref, weights, v_refs, scale=None):
  for h in range(N_HEADS):
    upd = acc_ref[h]
    if scale is not None:
      upd = upd * scale[h:h + 1, :]
    for w, v_ref in zip(weights, v_refs):
      upd = upd + w[h:h + 1, :] * v_ref[h]
    acc_ref[h] = upd


def _sb_decode_init(q_row, qrep, r_ref, acc_ref):
  qrep[...] = _lane_replicated(_row0(q_row))
  r_ref[...] = jnp.zeros(r_ref.shape, F32)
  acc_ref[...] = jnp.zeros(acc_ref.shape, F32)


def _sb_decode_pages(qrep, r_ref, acc_ref, uw_ref, k_pages, v_pages):
  weights = []
  r = r_ref[...]
  for z in _page_scores(qrep, k_pages):
    hi, lo = _split2(_neg_softplus(z))
    ct = _dot(jnp.concatenate([hi, lo], axis=1), uw_ref[...])
    weights.append(jnp.exp(z + ct[:, 0:BLK] + r))
    r = r + ct[:, BLK:2 * BLK]
  r_ref[...] = r
  _accumulate_values(acc_ref, weights, v_pages)


def _physical_pages(pool):
  return jnp.transpose(pool, (0, 2, 3, 1))


def _fox_decode_init(q_row, kn_row, vn_row, lfn, qrep, d_ref, m_ref, l_ref, acc_ref):
  qr = _lane_replicated(_row0(q_row))
  qrep[...] = qr
  m_ref[...] = jnp.sum(qr * _lane_replicated(_rows_split3(kn_row)), axis=1)
  l_ref[...] = jnp.ones(l_ref.shape, F32)
  lane = lax.broadcasted_iota(jnp.int32, acc_ref.shape, 2)
  acc_ref[...] = jnp.where(lane == 0, _lane_replicated(_rows_split3(vn_row)), 0.0)
  d_ref[...] = jnp.broadcast_to(lfn, d_ref.shape)


def _fox_decode_pages(qrep, d_ref, m_ref, l_ref, acc_ref, us_ref, k_pages, v_pages, f_pages):
  scores = []
  d = d_ref[...]
  for z, f_ref in zip(_page_scores(qrep, k_pages), f_pages):
    dt = _dot(jnp.concatenate(_split3(f_ref[...]), axis=1), us_ref[...])
    scores.append(z + dt[:, 0:BLK] + d)
    d = d + dt[:, BLK:2 * BLK]
  d_ref[...] = d
  m_old = m_ref[...]
  s_max = functools.reduce(jnp.maximum, scores)
  m_new = jnp.maximum(m_old, jnp.max(s_max, axis=1, keepdims=True))
  probs = [jnp.exp(s - m_new) for s in scores]
  alpha = jnp.exp(m_old - m_new)
  m_ref[...] = m_new
  l_ref[...] = alpha * l_ref[...] + jnp.sum(functools.reduce(jnp.add, probs), axis=1, keepdims=True)
  _accumulate_values(acc_ref, probs, v_pages, scale=alpha)


def _fox_decode_result(l_ref, acc_ref):
  return jnp.sum(acc_ref[...] * _over_dims(1.0 / l_ref[...]), axis=2, keepdims=True)


def _forget_suffix_weights():
  j = jnp.arange(BLK)
  us = jnp.concatenate([(j[:, None] > j[None, :]).astype(BF16), jnp.ones((BLK, BLK), BF16)], axis=1)
  return jnp.concatenate([us] * 3, axis=0)


def kernel(x_prompt, x_sample, cache_sb_k, cache_sb_v, cache_fox_k, cache_fox_v, cache_fox_logf,
           state_conv, page_table, meta_tokens, norm_mix, norm_ffn, w_qkv_sb, w_o_sb, w_in_fox,
           b_forget, q_norm_g, k_norm_g, w_o_fox, w_up, conv_w, conv_b, w_down, norm_final):
  b, seq, d = x_prompt.shape
  n = x_sample.shape[0]
  assert x_sample.shape[1] == 1 and d == HD
  t_len = N_META + seq
  depth = norm_mix.shape[0]
  assert depth == 2 and w_qkv_sb.shape[0] == 1 and w_in_fox.shape[0] == 1

  meta = jnp.broadcast_to(meta_tokens.astype(x_prompt.dtype)[None], (b, N_META, d))
  hp = jnp.concatenate([meta, x_prompt], axis=1)
  hs = x_sample.reshape(n, d)

  heads = lambda a, lead: a.reshape(lead + (N_HEADS, HEAD_DIM))[None]
  ffn = [_ffn_weights(w_up[i], conv_w[i], conv_b[i], w_down[i]) for i in range(depth)]
  conv_p, conv_s = [], []

  w = w_qkv_sb[0].astype(BF16)
  wo = w_o_sb[0].astype(BF16)
  q, k, v = _proj_sb(hp.reshape(b * t_len, d), norm_mix[0], w)
  q_s, k_s, v_s = _proj_sb(hs, norm_mix[0], w)
  sbk_s, sbv_s = heads(k_s, (n, 1)), heads(v_s, (n, 1))
  o, kt, vt, o_s = _sb_attn(q.reshape(b, t_len, HD), k.reshape(b, t_len, HD),
                            v.reshape(b, t_len, HD), q_s, cache_sb_k[0], cache_sb_v[0], page_table)
  sbk_p, sbv_p = _cache_order(kt), _cache_order(vt)
  hp, cp = _post_prompt(hp, o, wo, norm_ffn[0], ffn[0])
  conv_p.append(cp)
  hs, cs = _post_sample(hs, o_s, wo, norm_ffn[0], ffn[0], state_conv[0])
  conv_s.append(cs)

  w = w_in_fox[0, :, :3 * HD].astype(BF16)
  wf = jnp.pad(w_in_fox[0, :, 3 * HD:], ((0, 0), (0, LANES - N_HEADS))).astype(BF16)
  wo = w_o_fox[0].astype(BF16)
  fox_args = (norm_mix[1], w, wf, b_forget[0], q_norm_g[0], k_norm_g[0])
  q, k, v, lf = _proj_fox(hp.reshape(b * t_len, d), *fox_args)
  ff_p = lf.reshape(1, b, t_len, N_HEADS)
  q_s, k_s, v_s, lf_s = _proj_fox(hs, *fox_args)
  fk_s, fv_s = heads(k_s, (n, 1)), heads(v_s, (n, 1))
  ff_s = lf_s.reshape(1, n, 1, N_HEADS)
  o, kt, vt, o_s = _fox_attn(q.reshape(b, t_len, HD), k.reshape(b, t_len, HD),
                             v.reshape(b, t_len, HD), lf.reshape(b, t_len, N_HEADS),
                             q_s, k_s, v_s, lf_s,
                             cache_fox_k[0], cache_fox_v[0], cache_fox_logf[0], page_table)
  fk_p, fv_p = _cache_order(kt), _cache_order(vt)
  hp, cp, y_full = _post_prompt(hp, o, wo, norm_ffn[1], ffn[1], norm_final)
  conv_p.append(cp)
  hs, cs, y_s = _post_sample(hs, o_s, wo, norm_ffn[1], ffn[1], state_conv[1], norm_final)
  conv_s.append(cs)

  y_prompt = y_full[:, N_META:]
  y_sample = y_s.reshape(n, 1, d)
  return (y_prompt, y_sample, sbk_p, sbv_p, fk_p, fv_p, ff_p, jnp.stack(conv_p),
          sbk_s, sbv_s, fk_s, fv_s, ff_s, jnp.stack(conv_s))
```
